```python
import math
import jax, jax.numpy as jnp
from jax import lax
import numpy as np

D_MODEL = 1024
BATCH = 1
SEQ = 16384
DEPTH = 1
DEC_BATCH = 128
DEC_SEQ = 8
PAST_LEN = 8192
PAGE_SIZE = 128

MIX_WIDTH = D_MODEL
DA_HEADS = 4
DA_DIM = (MIX_WIDTH // 2) // (2 * DA_HEADS)
DA_OUT = DA_HEADS * 2 * DA_DIM
NSA_HEADS = 8
NSA_KV_GROUPS = 2
NSA_HPG = NSA_HEADS // NSA_KV_GROUPS
NSA_DIM = (MIX_WIDTH - DA_OUT) // NSA_HEADS
NSA_OUT = NSA_HEADS * NSA_DIM
CMP_STRIDE = 16
CMP_BLOCK = 2 * CMP_STRIDE
CMP_HIDDEN = 2 * NSA_DIM
SEL_BLOCK = 64
CMP_PER_SEL = SEL_BLOCK // CMP_STRIDE
SEL_TOPK = 16
WINDOW = 512
N_BRANCH = 3
ROPE_THETA = 500000.0
D_FF = 4 * D_MODEL
Q_BLOCK = 128
EPS = 1e-6
NEG_INF = -1e30
IN_SIZES = (DA_HEADS * 2 * DA_DIM,) * 3 + (NSA_HEADS * NSA_DIM,) + (NSA_KV_GROUPS * NSA_DIM,) * 6 + (NSA_HEADS * N_BRANCH,)
N_IN = sum(IN_SIZES)

kernel_name = "hymba_diffattn_nsa_step"


def _rms_norm(x, g):
    xf = x.astype(jnp.float32)
    y = xf * lax.rsqrt(jnp.mean(xf * xf, axis=-1, keepdims=True) + EPS)
    return (y * g.astype(jnp.float32)).astype(x.dtype)


def _partial_rope(x, pos):
    rot = x.shape[-1] // 4
    half = rot // 2
    inv = jnp.power(jnp.float32(ROPE_THETA), -jnp.arange(half, dtype=jnp.float32) * (2.0 / rot))
    ang = pos.astype(jnp.float32)[:, None] * inv[None, :]
    shape = (1, pos.shape[0]) + (1,) * (x.ndim - 3) + (half,)
    cos = jnp.cos(ang).reshape(shape)
    sin = jnp.sin(ang).reshape(shape)
    xr = x[..., :rot].astype(jnp.float32)
    x1, x2 = xr[..., :half], xr[..., half:]
    rotated = jnp.concatenate([x1 * cos - x2 * sin, x2 * cos + x1 * sin], axis=-1).astype(x.dtype)
    return jnp.concatenate([rotated, x[..., rot:]], axis=-1)


def _masked_softmax(s, mask):
    p = jax.nn.softmax(jnp.where(mask, s, NEG_INF), axis=-1)
    return jnp.where(mask, p, 0.0)


def _project(h, pos, w_in):
    n, t = h.shape[0], h.shape[1]
    cuts, acc = [], 0
    for s in IN_SIZES[:-1]:
        acc += s
        cuts.append(acc)
    qa, ka, va, qb, kc, vc, ks, vs, kw, vw, gl = jnp.split(h @ w_in, cuts, axis=-1)
    kvs = (n, t, NSA_KV_GROUPS, NSA_DIM)
    qa = _partial_rope(qa.reshape(n, t, DA_HEADS, 2, DA_DIM), pos)
    ka = _partial_rope(ka.reshape(n, t, DA_HEADS, 2, DA_DIM), pos)
    va = va.reshape(n, t, DA_HEADS, 2 * DA_DIM)
    qb = _partial_rope(qb.reshape(n, t, NSA_HEADS, NSA_DIM), pos)
    kc = _partial_rope(kc.reshape(kvs), pos)
    ks = _partial_rope(ks.reshape(kvs), pos)
    kw = _partial_rope(kw.reshape(kvs), pos)
    gates = jax.nn.sigmoid(gl.astype(jnp.float32)).reshape(n, t, NSA_HEADS, N_BRANCH).astype(h.dtype)
    return (qa, ka, va, qb, kc, vc.reshape(kvs), ks, vs.reshape(kvs), kw, vw.reshape(kvs), gates)


def _diff_attention(qa, qpos, ka, va, kpos, lam):
    s = jnp.einsum("qhcd,khcd->hcqk", qa, ka).astype(jnp.float32) * (DA_DIM ** -0.5)
    p = _masked_softmax(s, kpos[None, :] <= qpos[:, None])
    w = p[:, 0] - lam * p[:, 1]
    return jnp.einsum("hqk,khe->qhe", w.astype(va.dtype), va)


def _compress(rows, pe, w1, w2):
    g, d = rows.shape[1], rows.shape[2]
    half = rows.reshape(-1, CMP_STRIDE, g, d)
    blocks = jnp.concatenate([half[:-1], half[1:]], axis=1) + pe[None, :, None, :]
    flat = blocks.transpose(0, 2, 1, 3).reshape(blocks.shape[0], g, CMP_BLOCK * d)
    return jax.nn.silu(flat @ w1) @ w2


def _to_blocks(rows):
    nb = rows.shape[0] // SEL_BLOCK
    return rows.reshape(nb, SEL_BLOCK, rows.shape[1], rows.shape[2]).transpose(2, 0, 1, 3)


def _nsa_attention(q, qpos, ck, cv, ksb, vsb, kw, vw, kwpos, gates):
    nq = q.shape[0]
    scale = NSA_DIM ** -0.5
    qg = q.reshape(nq, NSA_KV_GROUPS, NSA_HPG, NSA_DIM)
    nc = ck.shape[0]
    cend = jnp.arange(nc) * CMP_STRIDE + (CMP_BLOCK - 1)
    cmask = (cend[None, :] <= qpos[:, None])[:, None, None, :]
    sc = jnp.einsum("qghd,cgd->qghc", qg, ck).astype(jnp.float32) * scale
    pc = _masked_softmax(sc, cmask)
    o_cmp = jnp.einsum("qghc,cgd->qghd", pc.astype(cv.dtype), cv)
    nb = ksb.shape[1]
    imp = jnp.pad(pc.sum(axis=2), ((0, 0), (0, 0), (0, nb * CMP_PER_SEL - nc)))
    imp = imp.reshape(nq, NSA_KV_GROUPS, nb, CMP_PER_SEL).sum(axis=-1)
    qblk = (qpos // SEL_BLOCK)[:, None, None]
    j = jnp.arange(nb)[None, None, :]
    forced = (j == 0) | (j == qblk) | (j == qblk - 1)
    score = jnp.where(forced, jnp.inf, jnp.where(j <= qblk, imp, -jnp.inf))
    _, idx = lax.top_k(score, min(SEL_TOPK, nb))
    sel_ok = idx <= qblk
    g_ar = jnp.arange(NSA_KV_GROUPS)[None, :, None]
    gk = ksb[g_ar, idx]
    gv = vsb[g_ar, idx]
    spos = idx[..., None] * SEL_BLOCK + jnp.arange(SEL_BLOCK)
    smask = ((spos <= qpos[:, None, None, None]) & sel_ok[..., None]).reshape(nq, NSA_KV_GROUPS, 1, -1)
    ss = jnp.einsum("qghd,qgnbd->qghnb", qg, gk).astype(jnp.float32) * scale
    ps = _masked_softmax(ss.reshape(nq, NSA_KV_GROUPS, NSA_HPG, -1), smask).reshape(ss.shape)
    o_sel = jnp.einsum("qghnb,qgnbd->qghd", ps.astype(gv.dtype), gv)
    wmask = (kwpos[None, :] <= qpos[:, None]) & (kwpos[None, :] > qpos[:, None] - WINDOW) & (kwpos[None, :] >= 0)
    sw = jnp.einsum("qghd,kgd->qghk", qg, kw).astype(jnp.float32) * scale
    pw = _masked_softmax(sw, wmask[:, None, None, :])
    o_win = jnp.einsum("qghk,kgd->qghd", pw.astype(vw.dtype), vw)
    g = gates.reshape(nq, NSA_KV_GROUPS, NSA_HPG, 1, N_BRANCH)
    o = jnp.stack([o_cmp, o_sel, o_win], axis=-1)
    return (o * g).sum(axis=-1).reshape(nq, NSA_OUT)


def _prompt_mix(seq, lam, cmpk, cmpv):
    qa, ka, va, qb, kc, vc, ks, vs, kw, vw, gates = seq
    t = qa.shape[0]
    kpos = jnp.arange(t)
    ck = _compress(kc, *cmpk)
    cv = _compress(vc, *cmpv)
    ksb = _to_blocks(ks)
    vsb = _to_blocks(vs)
    pad = ((WINDOW, 0), (0, 0), (0, 0))
    kw_pad = jnp.pad(kw, pad)
    vw_pad = jnp.pad(vw, pad)

    def block(start):
        qpos = start + jnp.arange(Q_BLOCK)
        o_a = _diff_attention(lax.dynamic_slice_in_dim(qa, start, Q_BLOCK, 0), qpos, ka, va, kpos, lam)
        kwpos = start - WINDOW + jnp.arange(WINDOW + Q_BLOCK)
        o_b = _nsa_attention(lax.dynamic_slice_in_dim(qb, start, Q_BLOCK, 0), qpos, ck, cv, ksb, vsb,
                             lax.dynamic_slice_in_dim(kw_pad, start, WINDOW + Q_BLOCK, 0),
                             lax.dynamic_slice_in_dim(vw_pad, start, WINDOW + Q_BLOCK, 0), kwpos,
                             lax.dynamic_slice_in_dim(gates, start, Q_BLOCK, 0))
        return o_a, o_b

    o_a, o_b = lax.map(block, jnp.arange(t // Q_BLOCK) * Q_BLOCK)
    return o_a.reshape((t,) + o_a.shape[2:]), o_b.reshape(t, NSA_OUT)


def _sample_mix(seq, layer, lam, caches, cmpk, cmpv):
    qa, ka, va, qb, kc, vc, ks, vs, kw, vw, gates, pages, win_k, win_v = seq
    c_da_k, c_da_v, c_ck, c_cv, c_sk, c_sv = caches
    t = qa.shape[0]
    total = PAST_LEN + t
    padn = -(-total // SEL_BLOCK) * SEL_BLOCK - total
    qpos = PAST_LEN + jnp.arange(t)

    def past_plus_new(cache, new, pad_rows):
        past = cache[layer, pages]
        past = past.reshape((past.shape[0] * past.shape[1],) + past.shape[2:])
        rows = jnp.concatenate([past, new], axis=0)
        return jnp.pad(rows, ((0, pad_rows),) + ((0, 0),) * (rows.ndim - 1))

    o_a = _diff_attention(qa, qpos, past_plus_new(c_da_k, ka, 0), past_plus_new(c_da_v, va, 0),
                          jnp.arange(total), lam)
    ck = _compress(past_plus_new(c_ck, kc, padn), *cmpk)
    cv = _compress(past_plus_new(c_cv, vc, padn), *cmpv)
    ksb = _to_blocks(past_plus_new(c_sk, ks, padn))
    vsb = _to_blocks(past_plus_new(c_sv, vs, padn))
    w_buf = win_k.shape[0]
    kw_all = jnp.concatenate([win_k, kw], axis=0)
    vw_all = jnp.concatenate([win_v, vw], axis=0)
    kwpos = PAST_LEN - w_buf + jnp.arange(w_buf + t)
    o_b = _nsa_attention(qb, qpos, ck, cv, ksb, vsb, kw_all, vw_all, kwpos, gates)
    return o_a, o_b, kw_all[t:], vw_all[t:]


def _finish(x, oa, ob, subln, lam_init, w_out, g_post, g_ffn_pre, w_up, w_down, g_ffn_post):
    n, t = x.shape[0], x.shape[1]
    oa = _rms_norm(oa, subln) * (1.0 - lam_init)
    mix = jnp.concatenate([oa.reshape(n, t, DA_OUT), ob], axis=-1) @ w_out
    x = x + _rms_norm(mix, g_post)
    h = _rms_norm(x, g_ffn_pre)
    f = jnp.square(jax.nn.relu(h @ w_up)) @ w_down
    return x + _rms_norm(f, g_ffn_post)


def setup_inputs(seed: int = 0) -> dict:
    key = jax.random.key(seed)
    k = jax.random.split(key, 32)
    f32 = jnp.float32
    n_pages = PAST_LEN // PAGE_SIZE
    n_used = DEC_BATCH * n_pages
    n_pool = n_used + max(1, n_used // 4)
    w_buf = min(WINDOW, PAST_LEN)

    def nrm(kk, shape, scale=1.0):
        return jax.random.normal(kk, shape, f32) * scale

    def gain(kk, width):
        return 1.0 + nrm(kk, (DEPTH, width), 0.02)

    kv_page = (DEPTH, n_pool, PAGE_SIZE, NSA_KV_GROUPS, NSA_DIM)
    win = (DEPTH, DEC_BATCH, w_buf, NSA_KV_GROUPS, NSA_DIM)
    page_table = jax.random.permutation(k[10], n_pool)[:n_used].reshape(DEC_BATCH, n_pages).astype(jnp.int32)
    return {
        "x_prompt": nrm(k[0], (BATCH, SEQ, D_MODEL)),
        "x_sample": nrm(k[1], (DEC_BATCH, DEC_SEQ, D_MODEL)),
        "cache_da_k": nrm(k[2], (DEPTH, n_pool, PAGE_SIZE, DA_HEADS, 2, DA_DIM)),
        "cache_da_v": nrm(k[3], (DEPTH, n_pool, PAGE_SIZE, DA_HEADS, 2 * DA_DIM)),
        "cache_nsa_cmp_k": nrm(k[4], kv_page),
        "cache_nsa_cmp_v": nrm(k[5], kv_page),
        "cache_nsa_sel_k": nrm(k[6], kv_page),
        "cache_nsa_sel_v": nrm(k[7], kv_page),
        "state_nsa_win_k": nrm(k[8], win),
        "state_nsa_win_v": nrm(k[9], win),
        "page_table": page_table,
        "norm_mix_pre": gain(k[11], D_MODEL),
        "norm_mix_post": gain(k[12], D_MODEL),
        "norm_ffn_pre": gain(k[13], D_MODEL),
        "norm_ffn_post": gain(k[14], D_MODEL),
        "w_in": nrm(k[15], (DEPTH, D_MODEL, N_IN), D_MODEL ** -0.5),
        "w_out": nrm(k[16], (DEPTH, MIX_WIDTH, D_MODEL), MIX_WIDTH ** -0.5),
        "da_lambda_q1": nrm(k[17], (DEPTH, DA_DIM), 0.1),
        "da_lambda_k1": nrm(k[18], (DEPTH, DA_DIM), 0.1),
        "da_lambda_q2": nrm(k[19], (DEPTH, DA_DIM), 0.1),
        "da_lambda_k2": nrm(k[20], (DEPTH, DA_DIM), 0.1),
        "da_subln": gain(k[21], 2 * DA_DIM),
        "cmp_pe_k": nrm(k[22], (DEPTH, CMP_BLOCK, NSA_DIM), 0.1),
        "cmp_w1_k": nrm(k[23], (DEPTH, CMP_BLOCK * NSA_DIM, CMP_HIDDEN), (CMP_BLOCK * NSA_DIM) ** -0.5),
        "cmp_w2_k": nrm(k[24], (DEPTH, CMP_HIDDEN, NSA_DIM), CMP_HIDDEN ** -0.5),
        "cmp_pe_v": nrm(k[25], (DEPTH, CMP_BLOCK, NSA_DIM), 0.1),
        "cmp_w1_v": nrm(k[26], (DEPTH, CMP_BLOCK * NSA_DIM, CMP_HIDDEN), (CMP_BLOCK * NSA_DIM) ** -0.5),
        "cmp_w2_v": nrm(k[27], (DEPTH, CMP_HIDDEN, NSA_DIM), CMP_HIDDEN ** -0.5),
        "w_up": nrm(k[28], (DEPTH, D_MODEL, D_FF), D_MODEL ** -0.5),
        "w_down": nrm(k[29], (DEPTH, D_FF, D_MODEL), D_FF ** -0.5),
    }


def reference(x_prompt, x_sample, cache_da_k, cache_da_v, cache_nsa_cmp_k, cache_nsa_cmp_v,
              cache_nsa_sel_k, cache_nsa_sel_v, state_nsa_win_k, state_nsa_win_v, page_table,
              norm_mix_pre, norm_mix_post, norm_ffn_pre, norm_ffn_post, w_in, w_out,
              da_lambda_q1, da_lambda_k1, da_lambda_q2, da_lambda_k2, da_subln,
              cmp_pe_k, cmp_w1_k, cmp_w2_k, cmp_pe_v, cmp_w1_v, cmp_w2_v, w_up, w_down):
    xp, xs = x_prompt, x_sample
    t_p, t_s = xp.shape[1], xs.shape[1]
    caches = (cache_da_k, cache_da_v, cache_nsa_cmp_k, cache_nsa_cmp_v, cache_nsa_sel_k, cache_nsa_sel_v)
    p_rows, s_rows = [], []
    for layer in range(DEPTH):
        lam_init = 0.8 - 0.6 * math.exp(-0.3 * layer)
        lam = (jnp.exp(jnp.sum(da_lambda_q1[layer].astype(jnp.float32) * da_lambda_k1[layer].astype(jnp.float32)))
               - jnp.exp(jnp.sum(da_lambda_q2[layer].astype(jnp.float32) * da_lambda_k2[layer].astype(jnp.float32)))
               + lam_init)
        cmpk = (cmp_pe_k[layer], cmp_w1_k[layer], cmp_w2_k[layer])
        cmpv = (cmp_pe_v[layer], cmp_w1_v[layer], cmp_w2_v[layer])
        post = (da_subln[layer], lam_init, w_out[layer], norm_mix_post[layer], norm_ffn_pre[layer],
                w_up[layer], w_down[layer], norm_ffn_post[layer])

        pp = _project(_rms_norm(xp, norm_mix_pre[layer]), jnp.arange(t_p), w_in[layer])
        oa_p, ob_p = lax.map(lambda s: _prompt_mix(s, lam, cmpk, cmpv), pp)
        w_keep = min(WINDOW, t_p)
        p_rows.append((pp[1], pp[2], pp[4], pp[5], pp[6], pp[7], pp[8][:, t_p - w_keep:], pp[9][:, t_p - w_keep:]))

        sp = _project(_rms_norm(xs, norm_mix_pre[layer]), PAST_LEN + jnp.arange(t_s), w_in[layer])
        oa_s, ob_s, nwk, nwv = lax.map(
            lambda s: _sample_mix(s, layer, lam, caches, cmpk, cmpv),
            sp + (page_table, state_nsa_win_k[layer], state_nsa_win_v[layer]))
        s_rows.append((sp[1], sp[2], sp[4], sp[5], sp[6], sp[7], nwk, nwv))

        xp = _finish(xp, oa_p, ob_p, *post)
        xs = _finish(xs, oa_s, ob_s, *post)

    p_da_k, p_da_v, p_cmp_k, p_cmp_v, p_sel_k, p_sel_v, p_win_k, p_win_v = [jnp.stack(list(c), axis=0) for c in zip(*p_rows)]
    s_da_k, s_da_v, s_cmp_k, s_cmp_v, s_sel_k, s_sel_v, s_win_k, s_win_v = [jnp.stack(list(c), axis=0) for c in zip(*s_rows)]
    return (xp, xs, p_da_k, p_da_v, p_cmp_k, p_cmp_v, p_sel_k, p_sel_v, p_win_k, p_win_v,
            s_da_k, s_da_v, s_cmp_k, s_cmp_v, s_sel_k, s_sel_v, s_win_k, s_win_v)
```

```python
import functools
import math

import jax
import jax.numpy as jnp
import numpy as np
from jax import lax
from jax.experimental import pallas as pl
from jax.experimental.pallas import tpu as pltpu

F32 = jnp.float32
BF16 = jnp.bfloat16

LANES = 128
D_MODEL = 1024
DA_HEADS = 4
DA_DIM = 64
DA_OUT = DA_HEADS * 2 * DA_DIM
NSA_HEADS = 8
NSA_GROUPS = 2
NSA_HPG = NSA_HEADS // NSA_GROUPS
NSA_DIM = 64
NSA_OUT = NSA_HEADS * NSA_DIM
KV_W = NSA_GROUPS * NSA_DIM
CMP_STRIDE = 16
CMP_BLOCK = 32
CMP_HIDDEN = 128
SEL_BLOCK = 64
CMP_PER_SEL = SEL_BLOCK // CMP_STRIDE
SEL_TOPK = 16
WINDOW = 512
N_BRANCH = 3
ROPE_THETA = 500000.0
ROPE_DIMS = 16
D_FF = 4 * D_MODEL
EPS = 1e-6
NEG = -1e30
PAGE = 128
SCALE = 0.125
HALF_W = CMP_STRIDE * KV_W

VMEM_LIMIT = 56 * 1024 * 1024

C_QA, C_KA, C_VA, C_QB = 0, 512, 1024, 1536
C_KV = 2560
C_GT = C_KV + 6 * KV_W
N_PROJ = C_GT + LANES


def _dot(a, b):
    return jnp.dot(a, b, preferred_element_type=F32)


def _dot_nt(a, b):
    return lax.dot_general(a, b, (((1,), (1,)), ((), ())), preferred_element_type=F32)


def _params(sem=None):
    return pltpu.CompilerParams(dimension_semantics=sem, vmem_limit_bytes=VMEM_LIMIT)


def _vmem_spec():
    return pl.BlockSpec(memory_space=pltpu.VMEM)


def _row_tile(rows, preferred):
    return preferred if rows % preferred == 0 else rows


def _prep_w_in(w):
    qa, ka, va = w[:, 0:512], w[:, 512:1024], w[:, 1024:1536]
    qb = w[:, 1536:2048].reshape(D_MODEL, NSA_HEADS, NSA_DIM)
    kv6 = w[:, 2048:2816]
    gl = w[:, 2816:2840]
    z = jnp.zeros((D_MODEL, NSA_DIM), w.dtype)
    qb_w = []
    for h in range(NSA_HEADS):
        qb_w += [qb[:, h], z] if h // NSA_HPG == 0 else [z, qb[:, h]]
    gl_p = jnp.concatenate([gl, jnp.zeros((D_MODEL, LANES - gl.shape[1]), w.dtype)], axis=1)
    return jnp.concatenate([qa, ka, va] + qb_w + [kv6, gl_p], axis=1).astype(BF16)


def _prep_cmp(pe, w1, w2):
    def expand(w1h):
        w = w1h.reshape(CMP_STRIDE, NSA_DIM, CMP_HIDDEN)
        z = jnp.zeros_like(w)
        g0 = jnp.concatenate([w, z], axis=-1)
        g1 = jnp.concatenate([z, w], axis=-1)
        return jnp.stack([g0, g1], axis=1).reshape(HALF_W, 2 * CMP_HIDDEN).astype(BF16)

    def tile_pe(p):
        return jnp.tile(p[:, None, :], (1, NSA_GROUPS, 1)).reshape(1, HALF_W).astype(F32)

    n = CMP_STRIDE * NSA_DIM
    z2 = jnp.zeros_like(w2)
    w2e = jnp.concatenate([jnp.concatenate([w2, z2], axis=1),
                           jnp.concatenate([z2, w2], axis=1)], axis=0).astype(BF16)
    return (tile_pe(pe[:CMP_STRIDE]), tile_pe(pe[CMP_STRIDE:]), expand(w1[:n]), expand(w1[n:]), w2e)


def _prep_w_out(w_out):
    wa = w_out[:DA_OUT]
    wb = w_out[DA_OUT:].reshape(NSA_HEADS, NSA_DIM, D_MODEL)
    order = [0, 4, 1, 5, 2, 6, 3, 7]
    wb = jnp.concatenate([wb[h] for h in order], axis=0)
    return wa.astype(BF16), wb.astype(BF16)


def _rope_tables(pos):
    half = ROPE_DIMS // 2
    inv = jnp.power(jnp.float32(ROPE_THETA), -jnp.arange(half, dtype=F32) * (2.0 / ROPE_DIMS))
    ang = pos.astype(F32)[:, None] * inv[None, :]
    cos, sin = jnp.cos(ang), jnp.sin(ang)
    n = pos.shape[0]
    one = jnp.ones((n, NSA_DIM - ROPE_DIMS), F32)
    zero = jnp.zeros((n, NSA_DIM - ROPE_DIMS), F32)
    z8 = jnp.zeros((n, half), F32)
    c = jnp.concatenate([cos, cos, one], axis=1)
    a = jnp.concatenate([-sin, z8, zero], axis=1)
    b = jnp.concatenate([z8, sin, zero], axis=1)
    return tuple(jnp.concatenate([t, t], axis=1) for t in (c, a, b))


def _expand_matrix_t(n_keys, n_blocks):
    k = np.arange(n_keys)[:, None] // SEL_BLOCK
    b = np.arange(n_blocks)[None, :]
    return jnp.asarray((k == b).astype(np.float32), dtype=BF16)


def _proj_kernel(x_ref, c_ref, a_ref, b_ref, g_ref, w_ref,
                 qa0_ref, qa1_ref, qb_ref, ka_ref, va_ref,
                 kc_ref, vc_ref, ks_ref, vs_ref, kw_ref, vw_ref, gt_ref,
                 kab_ref, vab_ref, ksb_ref, vsb_ref, kwb_ref, vwb_ref):
    x = x_ref[...]
    ms = jnp.mean(x * x, axis=-1, keepdims=True)
    h = (x * lax.rsqrt(ms + EPS) * g_ref[...]).astype(BF16)
    c, a, b = c_ref[...], a_ref[...], b_ref[...]

    def rope(z):
        return z * c + pltpu.roll(z, LANES - ROPE_DIMS // 2, 1) * a + pltpu.roll(z, ROPE_DIMS // 2, 1) * b

    def sect(col, width):
        return _dot(h, w_ref[:, col:col + width])

    lane = lax.broadcasted_iota(jnp.int32, (x.shape[0], LANES), 1)
    lo = lane < DA_DIM

    z = sect(C_QA, DA_OUT)
    for i in range(DA_HEADS):
        q = rope(z[:, i * LANES:(i + 1) * LANES]) * SCALE
        qa0_ref[:, i * LANES:(i + 1) * LANES] = jnp.where(lo, q, 0.0).astype(qa0_ref.dtype)
        qa1_ref[:, i * LANES:(i + 1) * LANES] = jnp.where(lo, 0.0, q).astype(qa1_ref.dtype)
    z = sect(C_KA, DA_OUT)
    for i in range(DA_HEADS):
        k = rope(z[:, i * LANES:(i + 1) * LANES])
        ka_ref[:, i * LANES:(i + 1) * LANES] = k
        kab_ref[:, i * LANES:(i + 1) * LANES] = k.astype(BF16)
    z = sect(C_VA, DA_OUT)
    va_ref[...] = z
    vab_ref[...] = z.astype(BF16)
    z = sect(C_QB, NSA_HEADS * LANES)
    for i in range(NSA_HEADS):
        qb_ref[:, i * LANES:(i + 1) * LANES] = (rope(z[:, i * LANES:(i + 1) * LANES]) * SCALE).astype(qb_ref.dtype)
    z = sect(C_KV, 6 * KV_W)
    kc_ref[...] = rope(z[:, 0:KV_W])
    vc_ref[...] = z[:, KV_W:2 * KV_W]
    k = rope(z[:, 2 * KV_W:3 * KV_W])
    ks_ref[...] = k
    ksb_ref[...] = k.astype(BF16)
    v = z[:, 3 * KV_W:4 * KV_W]
    vs_ref[...] = v
    vsb_ref[...] = v.astype(BF16)
    k = rope(z[:, 4 * KV_W:5 * KV_W])
    kw_ref[...] = k
    kwb_ref[...] = k.astype(BF16)
    v = z[:, 5 * KV_W:6 * KV_W]
    vw_ref[...] = v
    vwb_ref[...] = v.astype(BF16)
    gl = sect(C_GT, LANES)
    gt_ref[...] = 1.0 / (1.0 + jnp.exp(-gl))


def _project(x2d, pos, wp, g_pre, q_dtype, tm):
    rows = x2d.shape[0]
    tm = _row_tile(rows, tm)
    tabs = _rope_tables(pos)
    row_blk = lambda w: pl.BlockSpec((tm, w), lambda i: (i, 0))
    full = lambda shp: pl.BlockSpec(shp, lambda i: (0, 0))
    widths_f32 = [DA_OUT, DA_OUT] + [KV_W] * 6 + [LANES]
    widths_b16 = [DA_OUT, DA_OUT] + [KV_W] * 4
    out_shape = ([jax.ShapeDtypeStruct((rows, DA_OUT), q_dtype)] * 2
                 + [jax.ShapeDtypeStruct((rows, NSA_HEADS * LANES), q_dtype)]
                 + [jax.ShapeDtypeStruct((rows, w), F32) for w in widths_f32]
                 + [jax.ShapeDtypeStruct((rows, w), BF16) for w in widths_b16])
    out_specs = ([row_blk(DA_OUT)] * 2 + [row_blk(NSA_HEADS * LANES)]
                 + [row_blk(w) for w in widths_f32] + [row_blk(w) for w in widths_b16])
    outs = pl.pallas_call(
        _proj_kernel,
        grid=(rows // tm,),
        in_specs=[row_blk(D_MODEL), row_blk(LANES), row_blk(LANES), row_blk(LANES),
                  full((1, D_MODEL)), full((D_MODEL, N_PROJ))],
        out_specs=out_specs,
        out_shape=out_shape,
        compiler_params=_params(("arbitrary",)),
        name="proj",
    )(x2d, *tabs, g_pre.reshape(1, D_MODEL).astype(F32), wp)
    names = ("qa0", "qa1", "qb", "ka", "va", "kc", "vc", "ks", "vs", "kw", "vw", "gt",
             "ka_b", "va_b", "ks_b", "vs_b", "kw_b", "vw_b")
    return dict(zip(names, outs))


def _compress_halves(rows, pet, peb, w1t, w1b, w2):
    nh = rows.shape[0]
    top = _dot((rows + pet).astype(BF16), w1t)
    bot = _dot((rows + peb).astype(BF16), w1b)
    hid = top + pltpu.roll(bot, nh - 1, 0)
    act = hid * (1.0 / (1.0 + jnp.exp(-hid)))
    return _dot(act.astype(BF16), w2)


def _store_permuted(tok, tmp_ref, out_ref):
    nb = tok.shape[0] // CMP_PER_SEL
    tmp_ref[...] = tok
    for r in range(CMP_PER_SEL):
        out_ref[r * nb:(r + 1) * nb, :] = tmp_ref[pl.ds(r, nb, stride=CMP_PER_SEL), :].astype(out_ref.dtype)


def _cmp_prompt_kernel(rows_ref, pet_ref, peb_ref, w1t_ref, w1b_ref, w2_ref, out_ref, tmp_ref):
    tok = _compress_halves(rows_ref[...], pet_ref[...], peb_ref[...], w1t_ref[...], w1b_ref[...], w2_ref[...])
    _store_permuted(tok, tmp_ref, out_ref)


def _compress_prompt(rows, cw):
    nh = rows.shape[0] // CMP_STRIDE
    return pl.pallas_call(
        _cmp_prompt_kernel,
        in_specs=[_vmem_spec()] * 6,
        out_specs=_vmem_spec(),
        out_shape=jax.ShapeDtypeStruct((nh, KV_W), BF16),
        scratch_shapes=[pltpu.VMEM((nh, KV_W), F32)],
        compiler_params=_params(),
        name="cmp_prompt",
    )(rows.reshape(nh, HALF_W), *cw)


def _lambda(lq1, lk1, lq2, lk2, lam_init):
    return (jnp.exp(jnp.sum(lq1 * lk1, axis=-1, keepdims=True))
            - jnp.exp(jnp.sum(lq2 * lk2, axis=-1, keepdims=True)) + lam_init)


def _subln(o, g, lam_init):
    ms = jnp.mean(o * o, axis=-1, keepdims=True)
    return o * lax.rsqrt(ms + EPS) * g * (1.0 - lam_init)


def _da_prompt_kernel(lq1_ref, lk1_ref, lq2_ref, lk2_ref, sg_ref, q0_ref, q1_ref, k_ref, v_ref, o_ref,
                      m_sc, l_sc, acc_sc, *, tq, lam_init):
    qi = pl.program_id(1)
    lam = _lambda(lq1_ref[...], lk1_ref[...], lq2_ref[...], lk2_ref[...], lam_init)
    qs = (q0_ref[...], q1_ref[...])
    m_sc[...] = jnp.full(m_sc.shape, NEG, F32)
    l_sc[...] = jnp.zeros(l_sc.shape, F32)
    acc_sc[...] = jnp.zeros(acc_sc.shape, F32)

    def tile(kj, masked):
        start = pl.multiple_of(kj * tq, tq)
        k = k_ref[pl.ds(start, tq), :]
        v = v_ref[pl.ds(start, tq), :]
        for c in range(2):
            s = _dot_nt(qs[c], k)
            if masked:
                row = lax.broadcasted_iota(jnp.int32, s.shape, 0)
                col = lax.broadcasted_iota(jnp.int32, s.shape, 1)
                s = jnp.where(col <= row, s, NEG)
            m_old = m_sc[c]
            m_new = jnp.maximum(m_old, jnp.max(s, axis=-1, keepdims=True))
            alpha = jnp.exp(m_old - m_new)
            p = jnp.exp(s - m_new)
            l_sc[c] = alpha * l_sc[c] + jnp.sum(p, axis=-1, keepdims=True)
            acc_sc[c] = alpha * acc_sc[c] + _dot(p.astype(BF16), v)
            m_sc[c] = m_new

    def body(kj, carry):
        tile(kj, False)
        return carry

    lax.fori_loop(0, qi, body, 0)
    tile(qi, True)
    o = acc_sc[0] / l_sc[0] - lam * (acc_sc[1] / l_sc[1])
    o_ref[...] = _subln(o, sg_ref[...], lam_init)


def _da_prompt(p, lams, subln, lam_init, tq):
    seq = p["qa0"].shape[0]
    assert seq % tq == 0
    vec = lambda: pl.BlockSpec((1, DA_DIM), lambda h, i: (0, 0))
    qblk = pl.BlockSpec((tq, LANES), lambda h, i: (i, h))
    kvblk = pl.BlockSpec((seq, LANES), lambda h, i: (0, h))
    return pl.pallas_call(
        functools.partial(_da_prompt_kernel, tq=tq, lam_init=lam_init),
        grid=(DA_HEADS, seq // tq),
        in_specs=[vec(), vec(), vec(), vec(), pl.BlockSpec((1, LANES), lambda h, i: (0, 0)),
                  qblk, qblk, kvblk, kvblk],
        out_specs=qblk,
        out_shape=jax.ShapeDtypeStruct((seq, DA_OUT), F32),
        scratch_shapes=[pltpu.VMEM((2, tq, 1), F32), pltpu.VMEM((2, tq, 1), F32),
                        pltpu.VMEM((2, tq, LANES), F32)],
        compiler_params=_params(("arbitrary", "arbitrary")),
        name="da_prompt",
    )(*lams, subln, p["qa0"], p["qa1"], p["ka_b"], p["va_b"])


def _softmax_rows(s, mask):
    s = jnp.where(mask, s, NEG)
    m = jnp.max(s, axis=-1, keepdims=True)
    e = jnp.where(mask, jnp.exp(s - m), 0.0)
    l = jnp.sum(e, axis=-1, keepdims=True)
    return e * (1.0 / jnp.maximum(l, 1e-30))


def _select_blocks(imp, qblk):
    nb = imp.shape[1]
    jb = lax.broadcasted_iota(jnp.int32, imp.shape, 1)
    jbf = jb.astype(F32)
    forced = (jb == 0) | (jb == qblk) | (jb == qblk - 1)
    valid = jb <= qblk
    score = jnp.where(forced, 1e30, jnp.where(valid, imp, -1.0))
    picked = jnp.zeros(imp.shape, F32)
    for _ in range(min(SEL_TOPK, nb)):
        mx = jnp.max(score, axis=-1, keepdims=True)
        first = jnp.min(jnp.where(score == mx, jbf, 1e9), axis=-1, keepdims=True)
        hit = jbf == first
        picked = jnp.where(hit, 1.0, picked)
        score = jnp.where(hit, -2.0, score)
    return jnp.where(valid, picked, 0.0)


def _nsa_prompt_kernel(cend_ref, qb_ref, gt_ref, ck_ref, cv_ref, ks_ref, vs_ref, kw_ref, vw_ref, et_ref,
                       o_ref, m_sc, l_sc, acc_sc, res_sc, *, tq, tk):
    qi = pl.program_id(0)
    q0 = qi * tq
    nb = et_ref.shape[1]
    qpos = q0 + lax.broadcasted_iota(jnp.int32, (tq, 1), 0)
    gt = gt_ref[...]

    def q_head(h):
        return qb_ref[:, h * LANES:(h + 1) * LANES]

    def gate(h, br):
        i = h * N_BRANCH + br
        return gt[:, i:i + 1]

    ck = ck_ref[...]
    cv = cv_ref[...]
    cmask = cend_ref[...] <= qpos
    sels = []
    for g in range(NSA_GROUPS):
        psum = jnp.zeros((tq, ck.shape[0]), F32)
        for hh in range(NSA_HPG):
            h = g * NSA_HPG + hh
            p = _softmax_rows(_dot_nt(q_head(h), ck), cmask)
            psum = psum + p
            res_sc[h] = _dot(p.astype(BF16), cv) * gate(h, 0)
        imp = psum[:, 0:nb]
        for r in range(1, CMP_PER_SEL):
            imp = imp + psum[:, r * nb:(r + 1) * nb]
        sels.append(_select_blocks(imp, qpos // SEL_BLOCK).astype(BF16))

    m_sc[...] = jnp.full(m_sc.shape, NEG, F32)
    l_sc[...] = jnp.zeros(l_sc.shape, F32)
    acc_sc[...] = jnp.zeros(acc_sc.shape, F32)

    def sel_body(kj, carry):
        start = pl.multiple_of(kj * tk, tk)
        ks = ks_ref[pl.ds(start, tk), :]
        vs = vs_ref[pl.ds(start, tk), :]
        et = et_ref[pl.ds(start, tk), :]
        kpos = start + lax.broadcasted_iota(jnp.int32, (tq, tk), 1)
        causal = kpos <= qpos
        for g in range(NSA_GROUPS):
            allowed = (_dot_nt(sels[g], et) > 0.5) & causal
            for hh in range(NSA_HPG):
                h = g * NSA_HPG + hh
                s = jnp.where(allowed, _dot_nt(q_head(h), ks), NEG)
                m_old = m_sc[h]
                m_new = jnp.maximum(m_old, jnp.max(s, axis=-1, keepdims=True))
                alpha = jnp.exp(m_old - m_new)
                p = jnp.exp(s - m_new)
                l_sc[h] = alpha * l_sc[h] + jnp.sum(p, axis=-1, keepdims=True)
                acc_sc[h] = alpha * acc_sc[h] + _dot(p.astype(BF16), vs)
                m_sc[h] = m_new
        return carry

    lax.fori_loop(0, (q0 + tq + tk - 1) // tk, sel_body, 0)

    wlen = WINDOW + tq
    wstart = pl.multiple_of(jnp.maximum(q0 - WINDOW, 0), tq)
    kw = kw_ref[pl.ds(wstart, wlen), :]
    vw = vw_ref[pl.ds(wstart, wlen), :]
    kpos = wstart + lax.broadcasted_iota(jnp.int32, (tq, wlen), 1)
    wmask = (kpos <= qpos) & (kpos > qpos - WINDOW)
    lane = lax.broadcasted_iota(jnp.int32, (tq, LANES), 1)
    lo = lane < NSA_DIM
    for h in range(NSA_HEADS):
        pw = _softmax_rows(_dot_nt(q_head(h), kw), wmask)
        o_win = _dot(pw.astype(BF16), vw)
        o_sel = acc_sc[h] / l_sc[h]
        res_sc[h] = res_sc[h] + o_sel * gate(h, 1) + o_win * gate(h, 2)
    for h in range(NSA_HPG):
        o_ref[:, h * LANES:(h + 1) * LANES] = jnp.where(lo, res_sc[h], res_sc[h + NSA_HPG])


def _nsa_prompt(p, ck, cv, tq, tk):
    seq = p["qb"].shape[0]
    nc = seq // CMP_STRIDE
    nb = seq // SEL_BLOCK
    assert seq % tk == 0 and tk % tq == 0 and WINDOW % tq == 0 and seq >= WINDOW + tq
    pp = np.arange(nc)
    cend = ((CMP_PER_SEL * (pp % nb) + pp // nb) * CMP_STRIDE + (CMP_BLOCK - 1)).astype(np.int32)
    et = _expand_matrix_t(seq, nb)
    return pl.pallas_call(
        functools.partial(_nsa_prompt_kernel, tq=tq, tk=tk),
        grid=(seq // tq,),
        in_specs=[pl.BlockSpec((1, nc), lambda i: (0, 0)),
                  pl.BlockSpec((tq, NSA_HEADS * LANES), lambda i: (i, 0)),
                  pl.BlockSpec((tq, LANES), lambda i: (i, 0))] + [_vmem_spec()] * 7,
        out_specs=pl.BlockSpec((tq, NSA_OUT), lambda i: (i, 0)),
        out_shape=jax.ShapeDtypeStruct((seq, NSA_OUT), F32),
        scratch_shapes=[pltpu.VMEM((NSA_HEADS, tq, 1), F32), pltpu.VMEM((NSA_HEADS, tq, 1), F32),
                        pltpu.VMEM((NSA_HEADS, tq, LANES), F32), pltpu.VMEM((NSA_HEADS, tq, LANES), F32)],
        compiler_params=_params(("arbitrary",)),
        name="nsa_prompt",
    )(jnp.asarray(cend).reshape(1, nc), p["qb"], p["gt"], ck, cv, p["ks_b"], p["vs_b"], p["kw_b"], p["vw_b"], et)


def _page_copies(pt_ref, seq, first_page, n_pages, rows_per_page, slot, caches, bufs, sems):
    out = []
    for j in range(n_pages):
        pg = pt_ref[seq, first_page + j]
        for i, (cache, buf) in enumerate(zip(caches, bufs)):
            out.append(pltpu.make_async_copy(
                cache.at[pg], buf.at[slot, pl.ds(j * rows_per_page, rows_per_page)], sems.at[i, slot]))
    return out


def _online_update(state, s, v_b16):
    m_old, l_old, acc = state
    m_new = jnp.maximum(m_old, jnp.max(s, axis=-1, keepdims=True))
    alpha = jnp.exp(m_old - m_new)
    p = jnp.exp(s - m_new)
    l_new = alpha * l_old + jnp.sum(p, axis=-1, keepdims=True)
    return m_new, l_new, alpha * acc + _dot(p.astype(BF16), v_b16)


def _pad_rows(x, rows):
    return jnp.concatenate([x, jnp.zeros((rows - x.shape[0], x.shape[1]), x.dtype)], axis=0)


def _da_sample_kernel(pt_ref, lq1_ref, lk1_ref, lq2_ref, lk2_ref, sg_ref, q0_ref, q1_ref, kn_ref, vn_ref,
                      kc_hbm, vc_hbm, o_ref, kbuf, vbuf, sems, *, ch, nch, lam_init):
    b = pl.program_id(0)
    nbatch = pl.num_programs(0)
    t = q0_ref.shape[0]
    lam = _lambda(lq1_ref[...], lk1_ref[...], lq2_ref[...], lk2_ref[...], lam_init)

    def copies(seq, c, slot):
        return _page_copies(pt_ref, seq, c * ch, ch, PAGE, slot, (kc_hbm, vc_hbm), (kbuf, vbuf), sems)

    @pl.when(b == 0)
    def _():
        for cp in copies(0, 0, 0):
            cp.start()

    lane = lax.broadcasted_iota(jnp.int32, (t, DA_OUT), 1)
    parts = []
    for h in range(DA_HEADS):
        in_head = (lane >= h * LANES) & (lane < (h + 1) * LANES)
        parts += [jnp.where(in_head, q0_ref[...], 0.0), jnp.where(in_head, q1_ref[...], 0.0)]
    qbd = jnp.concatenate(parts, axis=0).astype(BF16)
    nrow = qbd.shape[0]

    state = (jnp.full((nrow, 1), NEG, F32), jnp.zeros((nrow, 1), F32), jnp.zeros((nrow, DA_OUT), F32))
    for c in range(nch):
        slot = c % 2
        if c + 1 < nch:
            for cp in copies(b, c + 1, (c + 1) % 2):
                cp.start()
        else:
            @pl.when(b + 1 < nbatch)
            def _():
                for cp in copies(b + 1, 0, 0):
                    cp.start()
        for cp in copies(b, c, slot):
            cp.wait()
        state = _online_update(state, _dot_nt(qbd, kbuf[slot].astype(BF16)), vbuf[slot].astype(BF16))

    kn = _pad_rows(kn_ref[...], LANES).astype(BF16)
    vn = _pad_rows(vn_ref[...], LANES).astype(BF16)
    col = lax.broadcasted_iota(jnp.int32, (nrow, LANES), 1)
    rowq = lax.broadcasted_iota(jnp.int32, (nrow, LANES), 0) & (t - 1)
    s_n = jnp.where((col < t) & (col <= rowq), _dot_nt(qbd, kn), NEG)
    _, l, acc = _online_update(state, s_n, vn)
    out = acc / l
    for h in range(DA_HEADS):
        r = h * 2 * t
        o = out[r:r + t, h * LANES:(h + 1) * LANES] - lam * out[r + t:r + 2 * t, h * LANES:(h + 1) * LANES]
        o_ref[:, h * LANES:(h + 1) * LANES] = _subln(o, sg_ref[...], lam_init)


def _da_sample(s, page_table, cache_k, cache_v, lams, subln, lam_init, t):
    nbatch, n_pages = page_table.shape
    assert t == 8 and n_pages % 2 == 0
    ch = min(16, n_pages // 2)
    nch = n_pages // ch
    assert nch % 2 == 0 and nch * ch == n_pages
    vec = lambda w: pl.BlockSpec((1, w), lambda b, pt: (0, 0))
    blk = pl.BlockSpec((t, DA_OUT), lambda b, pt: (b, 0))
    anyspec = pl.BlockSpec(memory_space=pl.ANY)
    grid_spec = pltpu.PrefetchScalarGridSpec(
        num_scalar_prefetch=1,
        grid=(nbatch,),
        in_specs=[vec(DA_DIM)] * 4 + [vec(LANES), blk, blk, blk, blk, anyspec, anyspec],
        out_specs=blk,
        scratch_shapes=[pltpu.VMEM((2, ch * PAGE, DA_OUT), F32), pltpu.VMEM((2, ch * PAGE, DA_OUT), F32),
                        pltpu.SemaphoreType.DMA((2, 2))],
    )
    return pl.pallas_call(
        functools.partial(_da_sample_kernel, ch=ch, nch=nch, lam_init=lam_init),
        grid_spec=grid_spec,
        out_shape=jax.ShapeDtypeStruct((nbatch * t, DA_OUT), F32),
        compiler_params=_params(("arbitrary",)),
        name="da_sample",
    )(page_table, *lams, subln, s["qa0"], s["qa1"], s["ka"], s["va"], cache_k, cache_v)


def _cmp_sample_kernel(pt_ref, petk_ref, pebk_ref, w1tk_ref, w1bk_ref, w2k_ref,
                       petv_ref, pebv_ref, w1tv_ref, w1bv_ref, w2v_ref, kc_hbm, vc_hbm,
                       ck_ref, cv_ref, kbuf, vbuf, tmp_ref, sems, *, n_pages):
    b = pl.program_id(0)
    nbatch = pl.num_programs(0)
    slot = b % 2
    halves_per_page = PAGE // CMP_STRIDE

    def copies(seq, sl):
        return _page_copies(pt_ref, seq, 0, n_pages, halves_per_page, sl, (kc_hbm, vc_hbm), (kbuf, vbuf), sems)

    @pl.when(b == 0)
    def _():
        for cp in copies(0, 0):
            cp.start()

    @pl.when(b + 1 < nbatch)
    def _():
        for cp in copies(b + 1, 1 - slot):
            cp.start()

    for cp in copies(b, slot):
        cp.wait()
    tok = _compress_halves(kbuf[slot], petk_ref[...], pebk_ref[...], w1tk_ref[...], w1bk_ref[...], w2k_ref[...])
    _store_permuted(tok, tmp_ref, ck_ref)
    tok = _compress_halves(vbuf[slot], petv_ref[...], pebv_ref[...], w1tv_ref[...], w1bv_ref[...], w2v_ref[...])
    _store_permuted(tok, tmp_ref, cv_ref)


def _compress_sample(page_table, cache_k, cache_v, cwk, cwv):
    nbatch, n_pages = page_table.shape
    nh = n_pages * PAGE // CMP_STRIDE
    out_blk = pl.BlockSpec((None, nh, KV_W), lambda b, pt: (b, 0, 0))
    anyspec = pl.BlockSpec(memory_space=pl.ANY)
    grid_spec = pltpu.PrefetchScalarGridSpec(
        num_scalar_prefetch=1,
        grid=(nbatch,),
        in_specs=[_vmem_spec()] * 10 + [anyspec, anyspec],
        out_specs=[out_blk, out_blk],
        scratch_shapes=[pltpu.VMEM((2, nh, HALF_W), F32), pltpu.VMEM((2, nh, HALF_W), F32),
                        pltpu.VMEM((nh, KV_W), F32), pltpu.SemaphoreType.DMA((2, 2))],
    )
    n_pool = cache_k.shape[0]
    view = lambda c: c.reshape(n_pool, PAGE // CMP_STRIDE, HALF_W)
    return pl.pallas_call(
        functools.partial(_cmp_sample_kernel, n_pages=n_pages),
        grid_spec=grid_spec,
        out_shape=[jax.ShapeDtypeStruct((nbatch, nh, KV_W), BF16)] * 2,
        compiler_params=_params(("arbitrary",)),
        name="cmp_sample",
    )(page_table, *cwk, *cwv, view(cache_k), view(cache_v))


def _nsa_sample_kernel(pt_ref, cend_ref, qb_ref, gt_ref, ck_ref, cv_ref, ksn_ref, vsn_ref, kwn_ref, vwn_ref,
                       wk_ref, wv_ref, et_ref, ks_hbm, vs_hbm, o_ref, kbuf, vbuf, sems, *, n_pages, past, chunk):
    b = pl.program_id(0)
    nbatch = pl.num_programs(0)
    slot = b % 2
    t = qb_ref.shape[0]
    nrow = NSA_HEADS * t
    nbp = past // SEL_BLOCK
    nbl = et_ref.shape[1]

    def copies(seq, sl):
        return _page_copies(pt_ref, seq, 0, n_pages, PAGE, sl, (ks_hbm, vs_hbm), (kbuf, vbuf), sems)

    @pl.when(b == 0)
    def _():
        for cp in copies(0, 0):
            cp.start()

    @pl.when(b + 1 < nbatch)
    def _():
        for cp in copies(b + 1, 1 - slot):
            cp.start()

    qn = jnp.concatenate([qb_ref[:, h * LANES:(h + 1) * LANES] for h in range(NSA_HEADS)], axis=0).astype(BF16)
    rowq = lax.broadcasted_iota(jnp.int32, (nrow, 1), 0) & (t - 1)
    qpos = past + rowq
    gt = gt_ref[...]

    def gate(br):
        return jnp.concatenate([gt[:, h * N_BRANCH + br:h * N_BRANCH + br + 1] for h in range(NSA_HEADS)], axis=0)

    p = _softmax_rows(_dot_nt(qn, ck_ref[...]), cend_ref[...] <= qpos)
    res = _dot(p.astype(BF16), cv_ref[...]) * gate(0)
    imps = []
    for g in range(NSA_GROUPS):
        psum = p[g * NSA_HPG * t:g * NSA_HPG * t + t]
        for hh in range(1, NSA_HPG):
            r = (g * NSA_HPG + hh) * t
            psum = psum + p[r:r + t]
        imp = psum[:, 0:nbp]
        for r in range(1, CMP_PER_SEL):
            imp = imp + psum[:, r * nbp:(r + 1) * nbp]
        imps.append(jnp.concatenate([imp, jnp.zeros((t, nbl - nbp), F32)], axis=1))
    q16 = past + (lax.broadcasted_iota(jnp.int32, (NSA_GROUPS * t, 1), 0) & (t - 1))
    sel = _select_blocks(jnp.concatenate(imps, axis=0), q16 // SEL_BLOCK)
    sel_rows = jnp.concatenate([sel[0:t]] * NSA_HPG + [sel[t:2 * t]] * NSA_HPG, axis=0)
    sel_b16 = sel_rows.astype(BF16)

    for cp in copies(b, slot):
        cp.wait()
    state = (jnp.full((nrow, 1), NEG, F32), jnp.zeros((nrow, 1), F32), jnp.zeros((nrow, KV_W), F32))
    for c in range(past // chunk):
        ks = kbuf[slot, pl.ds(c * chunk, chunk), :].astype(BF16)
        vs = vbuf[slot, pl.ds(c * chunk, chunk), :].astype(BF16)
        allowed = _dot_nt(sel_b16, et_ref[c * chunk:(c + 1) * chunk, :]) > 0.5
        state = _online_update(state, jnp.where(allowed, _dot_nt(qn, ks), NEG), vs)
    col = lax.broadcasted_iota(jnp.int32, (nrow, LANES), 1)
    new_ok = (col < t) & (col <= rowq)
    s_n = _dot_nt(qn, _pad_rows(ksn_ref[...], LANES).astype(BF16))
    s_n = jnp.where(new_ok & (sel_rows[:, nbp:nbp + 1] > 0.5), s_n, NEG)
    _, l, acc = _online_update(state, s_n, _pad_rows(vsn_ref[...], LANES).astype(BF16))
    res = res + (acc / l) * gate(1)

    wlen = wk_ref.shape[0]
    s_w = _dot_nt(qn, wk_ref[...].astype(BF16))
    wcol = lax.broadcasted_iota(jnp.int32, (nrow, wlen), 1)
    s_wn = _dot_nt(qn, _pad_rows(kwn_ref[...], LANES).astype(BF16))
    pw = _softmax_rows(jnp.concatenate([s_w, s_wn], axis=1),
                       jnp.concatenate([wcol > rowq + (wlen - WINDOW), new_ok], axis=1))
    o_win = (_dot(pw[:, :wlen].astype(BF16), wv_ref[...].astype(BF16))
             + _dot(pw[:, wlen:].astype(BF16), _pad_rows(vwn_ref[...], LANES).astype(BF16)))
    res = res + o_win * gate(2)

    lo = lax.broadcasted_iota(jnp.int32, (t, LANES), 1) < NSA_DIM
    for h in range(NSA_HPG):
        o_ref[:, h * LANES:(h + 1) * LANES] = jnp.where(lo, res[h * t:(h + 1) * t],
                                                        res[(h + NSA_HPG) * t:(h + NSA_HPG + 1) * t])


def _nsa_sample(s, ck, cv, page_table, cache_k, cache_v, win_k, win_v, past, t):
    nbatch, n_pages = page_table.shape
    nh = ck.shape[1]
    nbp = past // SEL_BLOCK
    nbl = -(-(nbp + 1) // LANES) * LANES
    wlen = win_k.shape[1]
    chunk = min(2048, past)
    assert t == 8 and past % chunk == 0 and past % SEL_BLOCK == 0 and wlen <= past and nh == CMP_PER_SEL * nbp
    pp = np.arange(nh)
    cend = ((CMP_PER_SEL * (pp % nbp) + pp // nbp) * CMP_STRIDE + (CMP_BLOCK - 1)).astype(np.int32)
    et = _expand_matrix_t(past, nbl)
    row = lambda w: pl.BlockSpec((t, w), lambda b, pt: (b, 0))
    per_seq = lambda n, w: pl.BlockSpec((None, n, w), lambda b, pt: (b, 0, 0))
    anyspec = pl.BlockSpec(memory_space=pl.ANY)
    grid_spec = pltpu.PrefetchScalarGridSpec(
        num_scalar_prefetch=1,
        grid=(nbatch,),
        in_specs=[pl.BlockSpec((1, nh), lambda b, pt: (0, 0)), row(NSA_HEADS * LANES), row(LANES),
                  per_seq(nh, KV_W), per_seq(nh, KV_W), row(KV_W), row(KV_W), row(KV_W), row(KV_W),
                  per_seq(wlen, KV_W), per_seq(wlen, KV_W), _vmem_spec(), anyspec, anyspec],
        out_specs=row(NSA_OUT),
        scratch_shapes=[pltpu.VMEM((2, past, KV_W), F32), pltpu.VMEM((2, past, KV_W), F32),
                        pltpu.SemaphoreType.DMA((2, 2))],
    )
    return pl.pallas_call(
        functools.partial(_nsa_sample_kernel, n_pages=n_pages, past=past, chunk=chunk),
        grid_spec=grid_spec,
        out_shape=jax.ShapeDtypeStruct((nbatch * t, NSA_OUT), F32),
        compiler_params=_params(("arbitrary",)),
        name="nsa_sample",
    )(page_table, jnp.asarray(cend).reshape(1, nh), s["qb"], s["gt"], ck, cv, s["ks"], s["vs"], s["kw"], s["vw"],
      win_k, win_v, et, cache_k, cache_v)


def _rms(x, g):
    return x * lax.rsqrt(jnp.mean(x * x, axis=-1, keepdims=True) + EPS) * g


def _finish_kernel(x_ref, oa_ref, ob_ref, gp_ref, gf_ref, gq_ref, wa_ref, wb_ref, wu_ref, wd_ref, y_ref):
    mix = _dot(oa_ref[...].astype(BF16), wa_ref[...]) + _dot(ob_ref[...].astype(BF16), wb_ref[...])
    x1 = x_ref[...] + _rms(mix, gp_ref[...])
    h = _rms(x1, gf_ref[...]).astype(BF16)
    u = jnp.maximum(_dot(h, wu_ref[...]), 0.0)
    f = _dot((u * u).astype(BF16), wd_ref[...])
    y_ref[...] = x1 + _rms(f, gq_ref[...])


def _finish(x2d, oa, ob, g_post, g_ffn_pre, g_ffn_post, wa, wb, wu, wd, tm):
    rows = x2d.shape[0]
    tm = _row_tile(rows, tm)
    row = lambda w: pl.BlockSpec((tm, w), lambda i: (i, 0))
    vec = pl.BlockSpec((1, D_MODEL), lambda i: (0, 0))
    g = lambda a: a.reshape(1, D_MODEL).astype(F32)
    return pl.pallas_call(
        _finish_kernel,
        grid=(rows // tm,),
        in_specs=[row(D_MODEL), row(DA_OUT), row(NSA_OUT), vec, vec, vec] + [_vmem_spec()] * 4,
        out_specs=row(D_MODEL),
        out_shape=jax.ShapeDtypeStruct((rows, D_MODEL), F32),
        compiler_params=_params(("arbitrary",)),
        name="finish",
    )(x2d, oa, ob, g(g_post), g(g_ffn_pre), g(g_ffn_post), wa, wb, wu, wd)


def kernel(x_prompt, x_sample, cache_da_k, cache_da_v, cache_nsa_cmp_k, cache_nsa_cmp_v, cache_nsa_sel_k,
           cache_nsa_sel_v, state_nsa_win_k, state_nsa_win_v, page_table, norm_mix_pre, norm_mix_post,
           norm_ffn_pre, norm_ffn_post, w_in, w_out, da_lambda_q1, da_lambda_k1, da_lambda_q2, da_lambda_k2,
           da_subln, cmp_pe_k, cmp_w1_k, cmp_w2_k, cmp_pe_v, cmp_w1_v, cmp_w2_v, w_up, w_down):
    depth = w_in.shape[0]
    n_p, seq = x_prompt.shape[:2]
    nbatch, t = x_sample.shape[:2]
    n_pool = cache_da_k.shape[1]
    past = page_table.shape[1] * PAGE
    assert n_p == 1 and cache_da_k.shape[2] == PAGE
    xp = x_prompt.reshape(seq, D_MODEL)
    xs = x_sample.reshape(nbatch * t, D_MODEL)
    pos_p = jnp.arange(seq)
    pos_s = past + jnp.tile(jnp.arange(t), nbatch)
    w_keep = min(WINDOW, seq)
    p_rows, s_rows = [], []
    for layer in range(depth):
        lam_init = 0.8 - 0.6 * math.exp(-0.3 * layer)
        lams = [a[layer].reshape(1, DA_DIM).astype(F32)
                for a in (da_lambda_q1, da_lambda_k1, da_lambda_q2, da_lambda_k2)]
        subln = da_subln[layer].reshape(1, LANES).astype(F32)
        wp = _prep_w_in(w_in[layer])
        cwk = _prep_cmp(cmp_pe_k[layer], cmp_w1_k[layer], cmp_w2_k[layer])
        cwv = _prep_cmp(cmp_pe_v[layer], cmp_w1_v[layer], cmp_w2_v[layer])
        wa, wb = _prep_w_out(w_out[layer])
        wu, wd = w_up[layer].astype(BF16), w_down[layer].astype(BF16)
        post = (norm_mix_post[layer], norm_ffn_pre[layer], norm_ffn_post[layer], wa, wb, wu, wd)

        p = _project(xp, pos_p, wp, norm_mix_pre[layer], BF16, 512)
        oa_p = _da_prompt(p, lams, subln, lam_init, 256)
        ck = _compress_prompt(p["kc"], cwk)
        cv = _compress_prompt(p["vc"], cwv)
        ob_p = _nsa_prompt(p, ck, cv, 128, 512)
        kv = lambda a: a.reshape(1, seq, NSA_GROUPS, NSA_DIM)
        p_rows.append((p["ka"].reshape(1, seq, DA_HEADS, 2, DA_DIM), p["va"].reshape(1, seq, DA_HEADS, 2 * DA_DIM),
                       kv(p["kc"]), kv(p["vc"]), kv(p["ks"]), kv(p["vs"]),
                       kv(p["kw"])[:, seq - w_keep:], kv(p["vw"])[:, seq - w_keep:]))

        s = _project(xs, pos_s, wp, norm_mix_pre[layer], F32, 512)
        oa_s = _da_sample(s, page_table, cache_da_k[layer].reshape(n_pool, PAGE, DA_OUT),
                          cache_da_v[layer].reshape(n_pool, PAGE, DA_OUT), lams, subln, lam_init, t)
        pk = lambda c: c[layer].reshape(n_pool, PAGE, KV_W)
        cks, cvs = _compress_sample(page_table, pk(cache_nsa_cmp_k), pk(cache_nsa_cmp_v), cwk, cwv)
        win_k = state_nsa_win_k[layer].reshape(nbatch, -1, KV_W)
        win_v = state_nsa_win_v[layer].reshape(nbatch, -1, KV_W)
        ob_s = _nsa_sample(s, cks, cvs, page_table, pk(cache_nsa_sel_k), pk(cache_nsa_sel_v), win_k, win_v, past, t)
        kvs = lambda a: a.reshape(nbatch, t, NSA_GROUPS, NSA_DIM)
        nwk = jnp.concatenate([state_nsa_win_k[layer], kvs(s["kw"])], axis=1)[:, t:]
        nwv = jnp.concatenate([state_nsa_win_v[layer], kvs(s["vw"])], axis=1)[:, t:]
        s_rows.append((s["ka"].reshape(nbatch, t, DA_HEADS, 2, DA_DIM), s["va"].reshape(nbatch, t, DA_HEADS, 2 * DA_DIM),
                       kvs(s["kc"]), kvs(s["vc"]), kvs(s["ks"]), kvs(s["vs"]), nwk, nwv))

        xp = _finish(xp, oa_p, ob_p, *post, 256)
        xs = _finish(xs, oa_s, ob_s, *post, 256)

    p_out = [jnp.stack(list(c), axis=0) for c in zip(*p_rows)]
    s_out = [jnp.stack(list(c), axis=0) for c in zip(*s_rows)]
    return (xp.reshape(1, seq, D_MODEL), xs.reshape(nbatch, t, D_MODEL), *p_out, *s_out)
```

```python
import functools
import math

import jax
import jax.numpy as jnp
import numpy as np
from jax import lax
from jax.experimental import pallas as pl
from jax.experimental.pallas import tpu as pltpu

F32 = jnp.float32
BF16 = jnp.bfloat16

LANES = 128
D_MODEL = 1024
DA_HEADS = 4
DA_DIM = 64
DA_OUT = DA_HEADS * 2 * DA_DIM
NSA_HEADS = 8
NSA_GROUPS = 2
NSA_HPG = NSA_HEADS // NSA_GROUPS
NSA_DIM = 64
NSA_OUT = NSA_HEADS * NSA_DIM
KV_W = NSA_GROUPS * NSA_DIM
CMP_STRIDE = 16
CMP_BLOCK = 32
CMP_HIDDEN = 128
SEL_BLOCK = 64
CMP_PER_SEL = SEL_BLOCK // CMP_STRIDE
SEL_TOPK = 16
WINDOW = 512
N_BRANCH = 3
ROPE_THETA = 500000.0
ROPE_DIMS = 16
D_FF = 4 * D_MODEL
EPS = 1e-6
NEG = -1e30
PAGE = 128
SCALE = 0.125
LOG2E = math.log2(math.e)
HALF_W = CMP_STRIDE * KV_W

VMEM_LIMIT = 56 * 1024 * 1024

C_QA, C_KA, C_VA, C_QB = 0, 512, 1024, 1536
C_KV = 2560
C_GT = C_KV + 6 * KV_W
N_PROJ = C_GT + LANES


def _dot(a, b):
    return jnp.dot(a, b, preferred_element_type=F32)


def _dot_nt(a, b):
    return lax.dot_general(a, b, (((1,), (1,)), ((), ())), preferred_element_type=F32)


def _params(sem=None):
    return pltpu.CompilerParams(dimension_semantics=sem, vmem_limit_bytes=VMEM_LIMIT)


def _vmem_spec():
    return pl.BlockSpec(memory_space=pltpu.VMEM)


def _row_tile(rows, preferred):
    return preferred if rows % preferred == 0 else rows


def _prep_w_in(w):
    qa, ka, va = w[:, 0:512], w[:, 512:1024], w[:, 1024:1536]
    qb = w[:, 1536:2048].reshape(D_MODEL, NSA_HEADS, NSA_DIM)
    kv6 = w[:, 2048:2816]
    gl = w[:, 2816:2840]
    z = jnp.zeros((D_MODEL, NSA_DIM), w.dtype)
    qb_w = []
    for h in range(NSA_HEADS):
        qb_w += [qb[:, h], z] if h // NSA_HPG == 0 else [z, qb[:, h]]
    gl_p = jnp.concatenate([gl, jnp.zeros((D_MODEL, LANES - gl.shape[1]), w.dtype)], axis=1)
    return jnp.concatenate([qa, ka, va] + qb_w + [kv6, gl_p], axis=1).astype(BF16)


def _prep_cmp(pe, w1, w2):
    def expand(w1h):
        w = w1h.reshape(CMP_STRIDE, NSA_DIM, CMP_HIDDEN)
        z = jnp.zeros_like(w)
        g0 = jnp.concatenate([w, z], axis=-1)
        g1 = jnp.concatenate([z, w], axis=-1)
        return jnp.stack([g0, g1], axis=1).reshape(HALF_W, 2 * CMP_HIDDEN).astype(BF16)

    def tile_pe(p):
        return jnp.tile(p[:, None, :], (1, NSA_GROUPS, 1)).reshape(1, HALF_W).astype(F32)

    n = CMP_STRIDE * NSA_DIM
    z2 = jnp.zeros_like(w2)
    w2e = jnp.concatenate([jnp.concatenate([w2, z2], axis=1),
                           jnp.concatenate([z2, w2], axis=1)], axis=0).astype(BF16)
    return (tile_pe(pe[:CMP_STRIDE]), tile_pe(pe[CMP_STRIDE:]), expand(w1[:n]), expand(w1[n:]), w2e)


def _prep_w_out(w_out):
    wa = w_out[:DA_OUT]
    wb = w_out[DA_OUT:].reshape(NSA_HEADS, NSA_DIM, D_MODEL)
    order = [0, 4, 1, 5, 2, 6, 3, 7]
    wb = jnp.concatenate([wb[h] for h in order], axis=0)
    return wa.astype(BF16), wb.astype(BF16)


def _rope_tables(pos):
    half = ROPE_DIMS // 2
    inv = jnp.power(jnp.float32(ROPE_THETA), -jnp.arange(half, dtype=F32) * (2.0 / ROPE_DIMS))
    ang = pos.astype(F32)[:, None] * inv[None, :]
    cos, sin = jnp.cos(ang), jnp.sin(ang)
    n = pos.shape[0]
    one = jnp.ones((n, NSA_DIM - ROPE_DIMS), F32)
    zero = jnp.zeros((n, NSA_DIM - ROPE_DIMS), F32)
    z8 = jnp.zeros((n, half), F32)
    c = jnp.concatenate([cos, cos, one], axis=1)
    a = jnp.concatenate([-sin, z8, zero], axis=1)
    b = jnp.concatenate([z8, sin, zero], axis=1)
    return tuple(jnp.concatenate([t, t], axis=1) for t in (c, a, b))


def _expand_matrix_t(n_keys, n_blocks):
    k = np.arange(n_keys)[:, None] // SEL_BLOCK
    b = np.arange(n_blocks)[None, :]
    return jnp.asarray((k == b).astype(np.float32), dtype=BF16)


def _proj_kernel(x_ref, c_ref, a_ref, b_ref, g_ref, w_ref,
                 qa0_ref, qa1_ref, qb_ref, ka_ref, va_ref,
                 kc_ref, vc_ref, ks_ref, vs_ref, kw_ref, vw_ref, gt_ref,
                 kab_ref, vab_ref, ksb_ref, vsb_ref, kwb_ref, vwb_ref):
    x = x_ref[...]
    ms = jnp.mean(x * x, axis=-1, keepdims=True)
    h = (x * lax.rsqrt(ms + EPS) * g_ref[...]).astype(BF16)
    c, a, b = c_ref[...], a_ref[...], b_ref[...]

    def rope(z):
        return z * c + pltpu.roll(z, LANES - ROPE_DIMS // 2, 1) * a + pltpu.roll(z, ROPE_DIMS // 2, 1) * b

    def sect(col, width):
        return _dot(h, w_ref[:, col:col + width])

    lane = lax.broadcasted_iota(jnp.int32, (x.shape[0], LANES), 1)
    lo = lane < DA_DIM

    z = sect(C_QA, DA_OUT)
    for i in range(DA_HEADS):
        q = rope(z[:, i * LANES:(i + 1) * LANES]) * SCALE
        qa0_ref[:, i * LANES:(i + 1) * LANES] = jnp.where(lo, q, 0.0).astype(qa0_ref.dtype)
        qa1_ref[:, i * LANES:(i + 1) * LANES] = jnp.where(lo, 0.0, q).astype(qa1_ref.dtype)
    z = sect(C_KA, DA_OUT)
    for i in range(DA_HEADS):
        k = rope(z[:, i * LANES:(i + 1) * LANES])
        ka_ref[:, i * LANES:(i + 1) * LANES] = k
        kab_ref[:, i * LANES:(i + 1) * LANES] = k.astype(BF16)
    z = sect(C_VA, DA_OUT)
    va_ref[...] = z
    vab_ref[...] = z.astype(BF16)
    z = sect(C_QB, NSA_HEADS * LANES)
    for i in range(NSA_HEADS):
        qb_ref[:, i * LANES:(i + 1) * LANES] = (rope(z[:, i * LANES:(i + 1) * LANES]) * SCALE).astype(qb_ref.dtype)
    z = sect(C_KV, 6 * KV_W)
    kc_ref[...] = rope(z[:, 0:KV_W])
    vc_ref[...] = z[:, KV_W:2 * KV_W]
    k = rope(z[:, 2 * KV_W:3 * KV_W])
    ks_ref[...] = k
    ksb_ref[...] = k.astype(BF16)
    v = z[:, 3 * KV_W:4 * KV_W]
    vs_ref[...] = v
    vsb_ref[...] = v.astype(BF16)
    k = rope(z[:, 4 * KV_W:5 * KV_W])
    kw_ref[...] = k
    kwb_ref[...] = k.astype(BF16)
    v = z[:, 5 * KV_W:6 * KV_W]
    vw_ref[...] = v
    vwb_ref[...] = v.astype(BF16)
    gl = sect(C_GT, LANES)
    gt_ref[...] = 1.0 / (1.0 + jnp.exp(-gl))


def _project(x2d, pos, wp, g_pre, q_dtype, tm):
    rows = x2d.shape[0]
    tm = _row_tile(rows, tm)
    tabs = _rope_tables(pos)
    row_blk = lambda w: pl.BlockSpec((tm, w), lambda i: (i, 0))
    full = lambda shp: pl.BlockSpec(shp, lambda i: (0, 0))
    widths_f32 = [DA_OUT, DA_OUT] + [KV_W] * 6 + [LANES]
    widths_b16 = [DA_OUT, DA_OUT] + [KV_W] * 4
    out_shape = ([jax.ShapeDtypeStruct((rows, DA_OUT), q_dtype)] * 2
                 + [jax.ShapeDtypeStruct((rows, NSA_HEADS * LANES), q_dtype)]
                 + [jax.ShapeDtypeStruct((rows, w), F32) for w in widths_f32]
                 + [jax.ShapeDtypeStruct((rows, w), BF16) for w in widths_b16])
    out_specs = ([row_blk(DA_OUT)] * 2 + [row_blk(NSA_HEADS * LANES)]
                 + [row_blk(w) for w in widths_f32] + [row_blk(w) for w in widths_b16])
    outs = pl.pallas_call(
        _proj_kernel,
        grid=(rows // tm,),
        in_specs=[row_blk(D_MODEL), row_blk(LANES), row_blk(LANES), row_blk(LANES),
                  full((1, D_MODEL)), full((D_MODEL, N_PROJ))],
        out_specs=out_specs,
        out_shape=out_shape,
        compiler_params=_params(("arbitrary",)),
        name="proj",
    )(x2d, *tabs, g_pre.reshape(1, D_MODEL).astype(F32), wp)
    names = ("qa0", "qa1", "qb", "ka", "va", "kc", "vc", "ks", "vs", "kw", "vw", "gt",
             "ka_b", "va_b", "ks_b", "vs_b", "kw_b", "vw_b")
    return dict(zip(names, outs))


R_QA, R_QB, R_KA, R_VA, R_KV = 0, 512, 1536, 2048, 2560
R_GT = R_KV + 6 * KV_W
N_GATES = NSA_HEADS * N_BRANCH
N_PROJ_T = R_GT + N_GATES


def _prep_w_in_t(w):
    wp = _prep_w_in(w)
    return jnp.concatenate([wp[:, C_QA:C_KA], wp[:, C_QB:C_KV], wp[:, C_KA:C_QB],
                            wp[:, C_KV:C_GT + N_GATES]], axis=1).T


def _rope_tables_t(pos):
    return tuple(tab.T for tab in _rope_tables(pos))


def _proj_t_kernel(x_ref, c_ref, a_ref, b_ref, g_ref, wt_ref,
                   qa0t_ref, qa1t_ref, qbt_ref, vat_b_ref, vst_b_ref, vwt_b_ref,
                   kat_ref, kct_ref, vct_ref, kst_ref, vst_ref, kwt_ref, vwt_ref, gtt_ref):
    x = x_ref[...]
    ms = jnp.mean(x * x, axis=-1, keepdims=True)
    h = (x * lax.rsqrt(ms + EPS) * g_ref[...]).astype(BF16)
    c, a, b = c_ref[...], a_ref[...], b_ref[...]

    def rope(z):
        return z * c + pltpu.roll(z, LANES - ROPE_DIMS // 2, 0) * a + pltpu.roll(z, ROPE_DIMS // 2, 0) * b

    def sect(r, n):
        return _dot_nt(wt_ref[r:r + n, :], h)

    def grp(z, i):
        return z[i * LANES:(i + 1) * LANES]

    lo = lax.broadcasted_iota(jnp.int32, (LANES, x.shape[0]), 0) < DA_DIM
    z = sect(R_QA, DA_OUT)
    for i in range(DA_HEADS):
        q = rope(grp(z, i)) * (SCALE * LOG2E)
        qa0t_ref[i * LANES:(i + 1) * LANES, :] = jnp.where(lo, q, 0.0).astype(BF16)
        qa1t_ref[i * LANES:(i + 1) * LANES, :] = jnp.where(lo, 0.0, q).astype(BF16)
    z = sect(R_QB, NSA_HEADS * LANES)
    for i in range(NSA_HEADS):
        qbt_ref[i * LANES:(i + 1) * LANES, :] = (rope(grp(z, i)) * SCALE).astype(BF16)
    z = sect(R_KA, DA_OUT)
    for i in range(DA_HEADS):
        kat_ref[i * LANES:(i + 1) * LANES, :] = rope(grp(z, i))
    vat_b_ref[...] = sect(R_VA, DA_OUT).astype(BF16)
    z = sect(R_KV, 6 * KV_W)
    kct_ref[...] = rope(grp(z, 0))
    vct_ref[...] = grp(z, 1)
    kst_ref[...] = rope(grp(z, 2))
    vst_ref[...] = grp(z, 3)
    vst_b_ref[...] = grp(z, 3).astype(BF16)
    kwt_ref[...] = rope(grp(z, 4))
    vwt_ref[...] = grp(z, 5)
    vwt_b_ref[...] = grp(z, 5).astype(BF16)
    gtt_ref[...] = 1.0 / (1.0 + jnp.exp(-sect(R_GT, N_GATES)))


def _project_t(x2d, pos, wpt, g_pre, tm):
    rows = x2d.shape[0]
    tm = _row_tile(rows, tm)
    tabs = _rope_tables_t(pos)
    col_blk = lambda n: pl.BlockSpec((n, tm), lambda i: (0, i))
    full = lambda shp: pl.BlockSpec(shp, lambda i: (0, 0))
    heights = ([(DA_OUT, BF16)] * 2 + [(NSA_HEADS * LANES, BF16), (DA_OUT, BF16), (KV_W, BF16), (KV_W, BF16),
                                      (DA_OUT, F32)] + [(KV_W, F32)] * 6 + [(N_GATES, F32)])
    outs = pl.pallas_call(
        _proj_t_kernel,
        grid=(rows // tm,),
        in_specs=[pl.BlockSpec((tm, D_MODEL), lambda i: (i, 0)), col_blk(LANES), col_blk(LANES), col_blk(LANES),
                  full((1, D_MODEL)), full((N_PROJ_T, D_MODEL))],
        out_specs=[col_blk(n) for n, _ in heights],
        out_shape=[jax.ShapeDtypeStruct((n, rows), dt) for n, dt in heights],
        compiler_params=_params(("arbitrary",)),
        name="proj_t",
    )(x2d, *tabs, g_pre.reshape(1, D_MODEL).astype(F32), wpt)
    names = ("qa0t", "qa1t", "qbt", "vat_b", "vst_b", "vwt_b", "kat", "kct", "vct", "kst", "vst", "kwt", "vwt", "gtt")
    return dict(zip(names, outs))


def _compress_halves(rows, pet, peb, w1t, w1b, w2):
    nh = rows.shape[0]
    top = _dot((rows + pet).astype(BF16), w1t)
    bot = _dot((rows + peb).astype(BF16), w1b)
    hid = top + pltpu.roll(bot, nh - 1, 0)
    act = hid * (1.0 / (1.0 + jnp.exp(-hid)))
    return _dot(act.astype(BF16), w2)


def _store_permuted(tok, tmp_ref, out_ref):
    nb = tok.shape[0] // CMP_PER_SEL
    tmp_ref[...] = tok
    for r in range(CMP_PER_SEL):
        out_ref[r * nb:(r + 1) * nb, :] = tmp_ref[pl.ds(r, nb, stride=CMP_PER_SEL), :].astype(out_ref.dtype)


def _cmp_prompt_kernel(rows_ref, pet_ref, peb_ref, w1t_ref, w1b_ref, w2_ref, out_ref, tmp_ref):
    tok = _compress_halves(rows_ref[...], pet_ref[...], peb_ref[...], w1t_ref[...], w1b_ref[...], w2_ref[...])
    _store_permuted(tok, tmp_ref, out_ref)


def _compress_prompt(rows, cw):
    nh = rows.shape[0] // CMP_STRIDE
    return pl.pallas_call(
        _cmp_prompt_kernel,
        in_specs=[_vmem_spec()] * 6,
        out_specs=_vmem_spec(),
        out_shape=jax.ShapeDtypeStruct((nh, KV_W), BF16),
        scratch_shapes=[pltpu.VMEM((nh, KV_W), F32)],
        compiler_params=_params(),
        name="cmp_prompt",
    )(rows.reshape(nh, HALF_W), *cw)


def _lambda(lq1, lk1, lq2, lk2, lam_init):
    return (jnp.exp(jnp.sum(lq1 * lk1, axis=-1, keepdims=True))
            - jnp.exp(jnp.sum(lq2 * lk2, axis=-1, keepdims=True)) + lam_init)


def _subln(o, g, lam_init):
    ms = jnp.mean(o * o, axis=-1, keepdims=True)
    return o * lax.rsqrt(ms + EPS) * g * (1.0 - lam_init)


ONES_ROWS = 16


def _da_prompt_kernel(lq1_ref, lk1_ref, lq2_ref, lk2_ref, sg_ref, q0t_ref, q1t_ref, k_ref, vt_ref, o_ref,
                      *scratch, tq, tk, nh, lam_init):
    m_scs, acc_scs = scratch[:nh], scratch[nh:]
    qi = pl.program_id(1)
    q0 = qi * tq
    lam = _lambda(lq1_ref[...], lk1_ref[...], lq2_ref[...], lk2_ref[...], lam_init)
    qcats = [jnp.concatenate([q0t_ref[hd * LANES:(hd + 1) * LANES, :], q1t_ref[hd * LANES:(hd + 1) * LANES, :]],
                             axis=1) for hd in range(nh)]
    for hd in range(nh):
        m_scs[hd][...] = jnp.full(m_scs[hd].shape, NEG, F32)
        acc_scs[hd][...] = jnp.zeros(acc_scs[hd].shape, F32)

    def tile(start, n, masked):
        for hd in range(nh):
            k = k_ref[pl.ds(start, n), hd * LANES:(hd + 1) * LANES]
            vt = jnp.concatenate([vt_ref[hd * LANES:(hd + 1) * LANES, pl.ds(start, n)],
                                  jnp.ones((ONES_ROWS, n), BF16)], axis=0)
            st = _dot(k, qcats[hd])
            if masked:
                kpos = start + lax.broadcasted_iota(jnp.int32, st.shape, 0)
                qpos = q0 + (lax.broadcasted_iota(jnp.int32, st.shape, 1) & (tq - 1))
                st = jnp.where(kpos <= qpos, st, NEG)
            m_old = m_scs[hd][...]
            m_new = jnp.maximum(m_old, jnp.max(st, axis=0, keepdims=True))
            alpha = jnp.exp2(m_old - m_new)
            pt = jnp.exp2(st - m_new[0:1]).astype(BF16)
            acc_scs[hd][...] = acc_scs[hd][...] * alpha[0:1] + _dot(vt, pt)
            m_scs[hd][...] = m_new

    def body(kj, carry):
        tile(pl.multiple_of(kj * tk, tk), tk, False)
        return carry

    n_big = q0 // tk
    lax.fori_loop(0, n_big, body, 0)
    for d in range(tk // tq - 1):
        @pl.when(q0 - n_big * tk > d * tq)
        def _():
            tile(pl.multiple_of(n_big * tk + d * tq, tq), tq, False)
    tile(pl.multiple_of(q0, tq), tq, True)
    for hd in range(nh):
        a0, a1 = acc_scs[hd][:, :tq], acc_scs[hd][:, tq:]
        ot = a0[:LANES] / a0[LANES:LANES + 1] - lam * (a1[:LANES] / a1[LANES:LANES + 1])
        ms = jnp.mean(ot * ot, axis=0, keepdims=True)
        yt = ot * lax.rsqrt(ms + EPS) * sg_ref[...] * (1.0 - lam_init)
        o_ref[:, hd * LANES:(hd + 1) * LANES] = yt.T


def _da_prompt(pt, ka_b, lams, subln, lam_init, tq, tk, nh):
    seq = ka_b.shape[0]
    assert seq % tq == 0 and tk % tq == 0 and tq & (tq - 1) == 0 and DA_HEADS % nh == 0
    w = nh * LANES
    vec = lambda: pl.BlockSpec((1, DA_DIM), lambda h, i: (0, 0))
    qblk = pl.BlockSpec((w, tq), lambda h, i: (h, i))
    sg = jnp.broadcast_to(subln.reshape(LANES, 1), (LANES, tq))
    return pl.pallas_call(
        functools.partial(_da_prompt_kernel, tq=tq, tk=tk, nh=nh, lam_init=lam_init),
        grid=(DA_HEADS // nh, seq // tq),
        in_specs=[vec(), vec(), vec(), vec(), pl.BlockSpec((LANES, tq), lambda h, i: (0, 0)),
                  qblk, qblk, pl.BlockSpec((seq, w), lambda h, i: (0, h)),
                  pl.BlockSpec((w, seq), lambda h, i: (h, 0))],
        out_specs=pl.BlockSpec((tq, w), lambda h, i: (i, h)),
        out_shape=jax.ShapeDtypeStruct((seq, DA_OUT), F32),
        scratch_shapes=([pltpu.VMEM((8, 2 * tq), F32)] * nh
                        + [pltpu.VMEM((LANES + ONES_ROWS, 2 * tq), F32)] * nh),
        compiler_params=_params(("arbitrary", "arbitrary")),
        name="da_prompt",
    )(*lams, sg, pt["qa0t"], pt["qa1t"], ka_b, pt["vat_b"])


def _softmax_rows(s, mask):
    s = jnp.where(mask, s, NEG)
    m = jnp.max(s, axis=-1, keepdims=True)
    e = jnp.where(mask, jnp.exp(s - m), 0.0)
    l = jnp.sum(e, axis=-1, keepdims=True)
    return e * (1.0 / jnp.maximum(l, 1e-30))


def _select_blocks(imp, qblk):
    nb = imp.shape[1]
    jb = lax.broadcasted_iota(jnp.int32, imp.shape, 1)
    jbf = jb.astype(F32)
    forced = (jb == 0) | (jb == qblk) | (jb == qblk - 1)
    valid = jb <= qblk
    score = jnp.where(forced, 1e30, jnp.where(valid, imp, -1.0))
    picked = jnp.zeros(imp.shape, F32)
    for _ in range(min(SEL_TOPK, nb)):
        mx = jnp.max(score, axis=-1, keepdims=True)
        first = jnp.min(jnp.where(score == mx, jbf, 1e9), axis=-1, keepdims=True)
        hit = jbf == first
        picked = jnp.where(hit, 1.0, picked)
        score = jnp.where(hit, -2.0, score)
    return jnp.where(valid, picked, 0.0)


def _nsa_prompt_kernel(cend_ref, qb_ref, gt_ref, ck_ref, cv_ref, ks_ref, vs_ref, kw_ref, vw_ref, et_ref,
                       o_ref, m_sc, l_sc, acc_sc, res_sc, *, tq, tk):
    qi = pl.program_id(0)
    q0 = qi * tq
    nb = et_ref.shape[1]
    qpos = q0 + lax.broadcasted_iota(jnp.int32, (tq, 1), 0)
    gt = gt_ref[...]

    def q_head(h):
        return qb_ref[:, h * LANES:(h + 1) * LANES]

    def gate(h, br):
        i = h * N_BRANCH + br
        return gt[:, i:i + 1]

    ck = ck_ref[...]
    cv = cv_ref[...]
    cmask = cend_ref[...] <= qpos
    sels = []
    for g in range(NSA_GROUPS):
        psum = jnp.zeros((tq, ck.shape[0]), F32)
        for hh in range(NSA_HPG):
            h = g * NSA_HPG + hh
            p = _softmax_rows(_dot_nt(q_head(h), ck), cmask)
            psum = psum + p
            res_sc[h] = _dot(p.astype(BF16), cv) * gate(h, 0)
        imp = psum[:, 0:nb]
        for r in range(1, CMP_PER_SEL):
            imp = imp + psum[:, r * nb:(r + 1) * nb]
        sels.append(_select_blocks(imp, qpos // SEL_BLOCK).astype(BF16))

    m_sc[...] = jnp.full(m_sc.shape, NEG, F32)
    l_sc[...] = jnp.zeros(l_sc.shape, F32)
    acc_sc[...] = jnp.zeros(acc_sc.shape, F32)

    def sel_body(kj, carry):
        start = pl.multiple_of(kj * tk, tk)
        ks = ks_ref[pl.ds(start, tk), :]
        vs = vs_ref[pl.ds(start, tk), :]
        et = et_ref[pl.ds(start, tk), :]
        kpos = start + lax.broadcasted_iota(jnp.int32, (tq, tk), 1)
        causal = kpos <= qpos
        for g in range(NSA_GROUPS):
            allowed = (_dot_nt(sels[g], et) > 0.5) & causal
            for hh in range(NSA_HPG):
                h = g * NSA_HPG + hh
                s = jnp.where(allowed, _dot_nt(q_head(h), ks), NEG)
                m_old = m_sc[h]
                m_new = jnp.maximum(m_old, jnp.max(s, axis=-1, keepdims=True))
                alpha = jnp.exp(m_old - m_new)
                p = jnp.exp(s - m_new)
                l_sc[h] = alpha * l_sc[h] + jnp.sum(p, axis=-1, keepdims=True)
                acc_sc[h] = alpha * acc_sc[h] + _dot(p.astype(BF16), vs)
                m_sc[h] = m_new
        return carry

    lax.fori_loop(0, (q0 + tq + tk - 1) // tk, sel_body, 0)

    wlen = WINDOW + tq
    wstart = pl.multiple_of(jnp.maximum(q0 - WINDOW, 0), tq)
    kw = kw_ref[pl.ds(wstart, wlen), :]
    vw = vw_ref[pl.ds(wstart, wlen), :]
    kpos = wstart + lax.broadcasted_iota(jnp.int32, (tq, wlen), 1)
    wmask = (kpos <= qpos) & (kpos > qpos - WINDOW)
    lane = lax.broadcasted_iota(jnp.int32, (tq, LANES), 1)
    lo = lane < NSA_DIM
    for h in range(NSA_HEADS):
        pw = _softmax_rows(_dot_nt(q_head(h), kw), wmask)
        o_win = _dot(pw.astype(BF16), vw)
        o_sel = acc_sc[h] / l_sc[h]
        res_sc[h] = res_sc[h] + o_sel * gate(h, 1) + o_win * gate(h, 2)
    for h in range(NSA_HPG):
        o_ref[:, h * LANES:(h + 1) * LANES] = jnp.where(lo, res_sc[h], res_sc[h + NSA_HPG])


def _nsa_prompt(p, ck, cv, tq, tk):
    seq = p["qb"].shape[0]
    nc = seq // CMP_STRIDE
    nb = seq // SEL_BLOCK
    assert seq % tk == 0 and tk % tq == 0 and WINDOW % tq == 0 and seq >= WINDOW + tq
    pp = np.arange(nc)
    cend = ((CMP_PER_SEL * (pp % nb) + pp // nb) * CMP_STRIDE + (CMP_BLOCK - 1)).astype(np.int32)
    et = _expand_matrix_t(seq, nb)
    return pl.pallas_call(
        functools.partial(_nsa_prompt_kernel, tq=tq, tk=tk),
        grid=(seq // tq,),
        in_specs=[pl.BlockSpec((1, nc), lambda i: (0, 0)),
                  pl.BlockSpec((tq, NSA_HEADS * LANES), lambda i: (i, 0)),
                  pl.BlockSpec((tq, LANES), lambda i: (i, 0))] + [_vmem_spec()] * 7,
        out_specs=pl.BlockSpec((tq, NSA_OUT), lambda i: (i, 0)),
        out_shape=jax.ShapeDtypeStruct((seq, NSA_OUT), F32),
        scratch_shapes=[pltpu.VMEM((NSA_HEADS, tq, 1), F32), pltpu.VMEM((NSA_HEADS, tq, 1), F32),
                        pltpu.VMEM((NSA_HEADS, tq, LANES), F32), pltpu.VMEM((NSA_HEADS, tq, LANES), F32)],
        compiler_params=_params(("arbitrary",)),
        name="nsa_prompt",
    )(jnp.asarray(cend).reshape(1, nc), p["qb"], p["gt"], ck, cv, p["ks_b"], p["vs_b"], p["kw_b"], p["vw_b"], et)


def _page_copies(pt_ref, seq, first_page, n_pages, slot, streams, sems):
    out = []
    for j in range(n_pages):
        pg = pt_ref[seq, first_page + j]
        for i, (cache, dst) in enumerate(streams):
            out.append(pltpu.make_async_copy(cache.at[pg], dst(slot, j), sems.at[i, slot]))
    return out


def _rows_dst(buf, rows):
    return lambda slot, j: buf.at[slot, pl.ds(j * rows, rows)]


def _cols_dst(buf, cols):
    return lambda slot, j: buf.at[slot, :, pl.ds(j * cols, cols)]


def _online_update(state, s, pv):
    m_old, l_old, acc = state
    m_new = jnp.maximum(m_old, jnp.max(s, axis=-1, keepdims=True))
    alpha = jnp.exp(m_old - m_new)
    p = jnp.exp(s - m_new)
    l_new = alpha * l_old + jnp.sum(p, axis=-1, keepdims=True)
    return m_new, l_new, alpha * acc + pv(p.astype(BF16))


def _pad_rows(x, rows):
    return jnp.concatenate([x, jnp.zeros((rows - x.shape[0], x.shape[1]), x.dtype)], axis=0)


def _da_sample_kernel(pt_ref, lq1_ref, lk1_ref, lq2_ref, lk2_ref, sg_ref, q0_ref, q1_ref, kn_ref, vn_ref,
                      kc_hbm, vc_hbm, o_ref, kbuf, vbuf, sems, *, ch, nch, lam_init):
    b = pl.program_id(0)
    nbatch = pl.num_programs(0)
    t = q0_ref.shape[0]
    lam = _lambda(lq1_ref[...], lk1_ref[...], lq2_ref[...], lk2_ref[...], lam_init)

    streams = ((kc_hbm, _cols_dst(kbuf, PAGE)), (vc_hbm, _rows_dst(vbuf, PAGE * DA_HEADS)))

    def copies(seq, c, slot):
        return _page_copies(pt_ref, seq, c * ch, ch, slot, streams, sems)

    @pl.when(b == 0)
    def _():
        for cp in copies(0, 0, 0):
            cp.start()

    lane = lax.broadcasted_iota(jnp.int32, (t, DA_OUT), 1)
    parts = []
    for h in range(DA_HEADS):
        in_head = (lane >= h * LANES) & (lane < (h + 1) * LANES)
        parts += [jnp.where(in_head, q0_ref[...], 0.0), jnp.where(in_head, q1_ref[...], 0.0)]
    qbd = jnp.concatenate(parts, axis=0).astype(BF16)
    nrow = qbd.shape[0]
    hrows = 2 * t

    state = (jnp.full((nrow, 1), NEG, F32), jnp.zeros((nrow, 1), F32), jnp.zeros((nrow, LANES), F32))
    for c in range(nch):
        slot = c % 2
        if c + 1 < nch:
            for cp in copies(b, c + 1, (c + 1) % 2):
                cp.start()
        else:
            @pl.when(b + 1 < nbatch)
            def _():
                for cp in copies(b + 1, 0, 0):
                    cp.start()
        for cp in copies(b, c, slot):
            cp.wait()

        def pv(p, slot=slot):
            return jnp.concatenate(
                [_dot(p[h * hrows:(h + 1) * hrows],
                      vbuf[slot, pl.ds(h, ch * PAGE, stride=DA_HEADS), :].astype(BF16))
                 for h in range(DA_HEADS)], axis=0)

        state = _online_update(state, _dot(qbd, kbuf[slot].astype(BF16)), pv)

    kn = _pad_rows(kn_ref[...], LANES).astype(BF16)
    vn = _pad_rows(vn_ref[...], LANES).astype(BF16)
    col = lax.broadcasted_iota(jnp.int32, (nrow, LANES), 1)
    rowq = lax.broadcasted_iota(jnp.int32, (nrow, LANES), 0) & (t - 1)
    s_n = jnp.where((col < t) & (col <= rowq), _dot_nt(qbd, kn), NEG)

    def pv_new(p):
        full = _dot(p, vn)
        return jnp.concatenate([full[h * hrows:(h + 1) * hrows, h * LANES:(h + 1) * LANES]
                                for h in range(DA_HEADS)], axis=0)

    _, l, acc = _online_update(state, s_n, pv_new)
    out = acc / l
    for h in range(DA_HEADS):
        r = h * hrows
        o = out[r:r + t] - lam * out[r + t:r + 2 * t]
        o_ref[:, h * LANES:(h + 1) * LANES] = _subln(o, sg_ref[...], lam_init)


def _da_sample(s, page_table, cache_k, cache_v, lams, subln, lam_init, t):
    nbatch, n_pages = page_table.shape
    assert t == 8 and n_pages % 2 == 0
    ch = min(16, n_pages // 2)
    nch = n_pages // ch
    assert nch % 2 == 0 and nch * ch == n_pages
    vec = lambda w: pl.BlockSpec((1, w), lambda b, pt: (0, 0))
    blk = pl.BlockSpec((t, DA_OUT), lambda b, pt: (b, 0))
    anyspec = pl.BlockSpec(memory_space=pl.ANY)
    grid_spec = pltpu.PrefetchScalarGridSpec(
        num_scalar_prefetch=1,
        grid=(nbatch,),
        in_specs=[vec(DA_DIM)] * 4 + [vec(LANES), blk, blk, blk, blk, anyspec, anyspec],
        out_specs=blk,
        scratch_shapes=[pltpu.VMEM((2, DA_OUT, ch * PAGE), F32), pltpu.VMEM((2, ch * PAGE * DA_HEADS, LANES), F32),
                        pltpu.SemaphoreType.DMA((2, 2))],
    )
    return pl.pallas_call(
        functools.partial(_da_sample_kernel, ch=ch, nch=nch, lam_init=lam_init),
        grid_spec=grid_spec,
        out_shape=jax.ShapeDtypeStruct((nbatch * t, DA_OUT), F32),
        compiler_params=_params(("arbitrary",)),
        name="da_sample",
    )(page_table, *lams, subln, s["qa0"], s["qa1"], s["ka"], s["va"], cache_k, cache_v)


def _cmp_sample_kernel(pt_ref, petk_ref, pebk_ref, w1tk_ref, w1bk_ref, w2k_ref,
                       petv_ref, pebv_ref, w1tv_ref, w1bv_ref, w2v_ref, kc_hbm, vc_hbm,
                       ck_ref, cv_ref, kbuf, vbuf, tmp_ref, sems, *, n_pages):
    b = pl.program_id(0)
    nbatch = pl.num_programs(0)
    slot = b % 2
    halves_per_page = PAGE // CMP_STRIDE

    streams = ((kc_hbm, _rows_dst(kbuf, halves_per_page)), (vc_hbm, _rows_dst(vbuf, halves_per_page)))

    def copies(seq, sl):
        return _page_copies(pt_ref, seq, 0, n_pages, sl, streams, sems)

    @pl.when(b == 0)
    def _():
        for cp in copies(0, 0):
            cp.start()

    @pl.when(b + 1 < nbatch)
    def _():
        for cp in copies(b + 1, 1 - slot):
            cp.start()

    for cp in copies(b, slot):
        cp.wait()
    tok = _compress_halves(kbuf[slot], petk_ref[...], pebk_ref[...], w1tk_ref[...], w1bk_ref[...], w2k_ref[...])
    _store_permuted(tok, tmp_ref, ck_ref)
    tok = _compress_halves(vbuf[slot], petv_ref[...], pebv_ref[...], w1tv_ref[...], w1bv_ref[...], w2v_ref[...])
    _store_permuted(tok, tmp_ref, cv_ref)


def _compress_sample(page_table, cache_k, cache_v, cwk, cwv):
    nbatch, n_pages = page_table.shape
    nh = n_pages * PAGE // CMP_STRIDE
    out_blk = pl.BlockSpec((None, nh, KV_W), lambda b, pt: (b, 0, 0))
    anyspec = pl.BlockSpec(memory_space=pl.ANY)
    grid_spec = pltpu.PrefetchScalarGridSpec(
        num_scalar_prefetch=1,
        grid=(nbatch,),
        in_specs=[_vmem_spec()] * 10 + [anyspec, anyspec],
        out_specs=[out_blk, out_blk],
        scratch_shapes=[pltpu.VMEM((2, nh, HALF_W), F32), pltpu.VMEM((2, nh, HALF_W), F32),
                        pltpu.VMEM((nh, KV_W), F32), pltpu.SemaphoreType.DMA((2, 2))],
    )
    n_pool = cache_k.shape[0]
    view = lambda c: c.reshape(n_pool, PAGE // CMP_STRIDE, HALF_W)
    return pl.pallas_call(
        functools.partial(_cmp_sample_kernel, n_pages=n_pages),
        grid_spec=grid_spec,
        out_shape=[jax.ShapeDtypeStruct((nbatch, nh, KV_W), BF16)] * 2,
        compiler_params=_params(("arbitrary",)),
        name="cmp_sample",
    )(page_table, *cwk, *cwv, view(cache_k), view(cache_v))


def _nsa_sample_kernel(pt_ref, cend_ref, qb_ref, gt_ref, ck_ref, cv_ref, ksn_ref, vsn_ref, kwn_ref, vwn_ref,
                       wk_ref, wv_ref, e_ref, ks_hbm, vs_hbm, o_ref, kbuf, vbuf, sems, *, n_pages, past, chunk):
    b = pl.program_id(0)
    nbatch = pl.num_programs(0)
    slot = b % 2
    t = qb_ref.shape[0]
    nrow = NSA_HEADS * t
    nbp = past // SEL_BLOCK
    nbl = e_ref.shape[0]
    streams = ((ks_hbm, _cols_dst(kbuf, PAGE)), (vs_hbm, _cols_dst(vbuf, PAGE)))

    def copies(seq, sl):
        return _page_copies(pt_ref, seq, 0, n_pages, sl, streams, sems)

    @pl.when(b == 0)
    def _():
        for cp in copies(0, 0):
            cp.start()

    @pl.when(b + 1 < nbatch)
    def _():
        for cp in copies(b + 1, 1 - slot):
            cp.start()

    qn = jnp.concatenate([qb_ref[:, h * LANES:(h + 1) * LANES] for h in range(NSA_HEADS)], axis=0).astype(BF16)
    rowq = lax.broadcasted_iota(jnp.int32, (nrow, 1), 0) & (t - 1)
    qpos = past + rowq
    gt = gt_ref[...]

    def gate(br):
        return jnp.concatenate([gt[:, h * N_BRANCH + br:h * N_BRANCH + br + 1] for h in range(NSA_HEADS)], axis=0)

    p = _softmax_rows(_dot_nt(qn, ck_ref[...]), cend_ref[...] <= qpos)
    res = _dot(p.astype(BF16), cv_ref[...]) * gate(0)
    imps = []
    for g in range(NSA_GROUPS):
        psum = p[g * NSA_HPG * t:g * NSA_HPG * t + t]
        for hh in range(1, NSA_HPG):
            r = (g * NSA_HPG + hh) * t
            psum = psum + p[r:r + t]
        imp = psum[:, 0:nbp]
        for r in range(1, CMP_PER_SEL):
            imp = imp + psum[:, r * nbp:(r + 1) * nbp]
        imps.append(jnp.concatenate([imp, jnp.zeros((t, nbl - nbp), F32)], axis=1))
    q16 = past + (lax.broadcasted_iota(jnp.int32, (NSA_GROUPS * t, 1), 0) & (t - 1))
    sel = _select_blocks(jnp.concatenate(imps, axis=0), q16 // SEL_BLOCK)
    sel_rows = jnp.concatenate([sel[0:t]] * NSA_HPG + [sel[t:2 * t]] * NSA_HPG, axis=0)
    sel_b16 = sel_rows.astype(BF16)

    for cp in copies(b, slot):
        cp.wait()
    state = (jnp.full((nrow, 1), NEG, F32), jnp.zeros((nrow, 1), F32), jnp.zeros((nrow, KV_W), F32))
    for c in range(past // chunk):
        kst = kbuf[slot, :, pl.ds(c * chunk, chunk)].astype(BF16)
        vst = vbuf[slot, :, pl.ds(c * chunk, chunk)].astype(BF16)
        allowed = _dot(sel_b16, e_ref[:, c * chunk:(c + 1) * chunk]) > 0.5
        state = _online_update(state, jnp.where(allowed, _dot(qn, kst), NEG),
                               lambda p, vst=vst: _dot_nt(p, vst))
    col = lax.broadcasted_iota(jnp.int32, (nrow, LANES), 1)
    new_ok = (col < t) & (col <= rowq)
    s_n = _dot_nt(qn, _pad_rows(ksn_ref[...], LANES).astype(BF16))
    s_n = jnp.where(new_ok & (sel_rows[:, nbp:nbp + 1] > 0.5), s_n, NEG)
    vsn = _pad_rows(vsn_ref[...], LANES).astype(BF16)
    _, l, acc = _online_update(state, s_n, lambda p: _dot(p, vsn))
    res = res + (acc / l) * gate(1)

    wlen = wk_ref.shape[1]
    s_w = _dot(qn, wk_ref[...].astype(BF16))
    wcol = lax.broadcasted_iota(jnp.int32, (nrow, wlen), 1)
    s_wn = _dot_nt(qn, _pad_rows(kwn_ref[...], LANES).astype(BF16))
    pw = _softmax_rows(jnp.concatenate([s_w, s_wn], axis=1),
                       jnp.concatenate([wcol > rowq + (wlen - WINDOW), new_ok], axis=1))
    o_win = (_dot_nt(pw[:, :wlen].astype(BF16), wv_ref[...].astype(BF16))
             + _dot(pw[:, wlen:].astype(BF16), _pad_rows(vwn_ref[...], LANES).astype(BF16)))
    res = res + o_win * gate(2)

    lo = lax.broadcasted_iota(jnp.int32, (t, LANES), 1) < NSA_DIM
    for h in range(NSA_HPG):
        o_ref[:, h * LANES:(h + 1) * LANES] = jnp.where(lo, res[h * t:(h + 1) * t],
                                                        res[(h + NSA_HPG) * t:(h + NSA_HPG + 1) * t])


def _nsa_sample(s, ck, cv, page_table, cache_k, cache_v, win_k, win_v, past, t):
    nbatch, n_pages = page_table.shape
    nh = ck.shape[1]
    nbp = past // SEL_BLOCK
    nbl = -(-(nbp + 1) // LANES) * LANES
    wlen = win_k.shape[2]
    chunk = min(2048, past)
    assert t == 8 and past % chunk == 0 and past % SEL_BLOCK == 0 and wlen <= past and nh == CMP_PER_SEL * nbp
    pp = np.arange(nh)
    cend = ((CMP_PER_SEL * (pp % nbp) + pp // nbp) * CMP_STRIDE + (CMP_BLOCK - 1)).astype(np.int32)
    e = _expand_matrix_t(past, nbl).T
    row = lambda w: pl.BlockSpec((t, w), lambda b, pt: (b, 0))
    per_seq = lambda n, w: pl.BlockSpec((None, n, w), lambda b, pt: (b, 0, 0))
    anyspec = pl.BlockSpec(memory_space=pl.ANY)
    grid_spec = pltpu.PrefetchScalarGridSpec(
        num_scalar_prefetch=1,
        grid=(nbatch,),
        in_specs=[pl.BlockSpec((1, nh), lambda b, pt: (0, 0)), row(NSA_HEADS * LANES), row(LANES),
                  per_seq(nh, KV_W), per_seq(nh, KV_W), row(KV_W), row(KV_W), row(KV_W), row(KV_W),
                  per_seq(KV_W, wlen), per_seq(KV_W, wlen), _vmem_spec(), anyspec, anyspec],
        out_specs=row(NSA_OUT),
        scratch_shapes=[pltpu.VMEM((2, KV_W, past), F32), pltpu.VMEM((2, KV_W, past), F32),
                        pltpu.SemaphoreType.DMA((2, 2))],
    )
    return pl.pallas_call(
        functools.partial(_nsa_sample_kernel, n_pages=n_pages, past=past, chunk=chunk),
        grid_spec=grid_spec,
        out_shape=jax.ShapeDtypeStruct((nbatch * t, NSA_OUT), F32),
        compiler_params=_params(("arbitrary",)),
        name="nsa_sample",
    )(page_table, jnp.asarray(cend).reshape(1, nh), s["qb"], s["gt"], ck, cv, s["ks"], s["vs"], s["kw"], s["vw"],
      win_k, win_v, e, cache_k, cache_v)


def _rms(x, g):
    return x * lax.rsqrt(jnp.mean(x * x, axis=-1, keepdims=True) + EPS) * g


def _finish_kernel(x_ref, oa_ref, ob_ref, gp_ref, gf_ref, gq_ref, wa_ref, wb_ref, wu_ref, wd_ref, y_ref):
    mix = _dot(oa_ref[...].astype(BF16), wa_ref[...]) + _dot(ob_ref[...].astype(BF16), wb_ref[...])
    x1 = x_ref[...] + _rms(mix, gp_ref[...])
    h = _rms(x1, gf_ref[...]).astype(BF16)
    u = jnp.maximum(_dot(h, wu_ref[...]), 0.0)
    f = _dot((u * u).astype(BF16), wd_ref[...])
    y_ref[...] = x1 + _rms(f, gq_ref[...])


def _finish(x2d, oa, ob, g_post, g_ffn_pre, g_ffn_post, wa, wb, wu, wd, tm):
    rows = x2d.shape[0]
    tm = _row_tile(rows, tm)
    row = lambda w: pl.BlockSpec((tm, w), lambda i: (i, 0))
    vec = pl.BlockSpec((1, D_MODEL), lambda i: (0, 0))
    g = lambda a: a.reshape(1, D_MODEL).astype(F32)
    return pl.pallas_call(
        _finish_kernel,
        grid=(rows // tm,),
        in_specs=[row(D_MODEL), row(DA_OUT), row(NSA_OUT), vec, vec, vec] + [_vmem_spec()] * 4,
        out_specs=row(D_MODEL),
        out_shape=jax.ShapeDtypeStruct((rows, D_MODEL), F32),
        compiler_params=_params(("arbitrary",)),
        name="finish",
    )(x2d, oa, ob, g(g_post), g(g_ffn_pre), g(g_ffn_post), wa, wb, wu, wd)


def kernel(x_prompt, x_sample, cache_da_k, cache_da_v, cache_nsa_cmp_k, cache_nsa_cmp_v, cache_nsa_sel_k,
           cache_nsa_sel_v, state_nsa_win_k, state_nsa_win_v, page_table, norm_mix_pre, norm_mix_post,
           norm_ffn_pre, norm_ffn_post, w_in, w_out, da_lambda_q1, da_lambda_k1, da_lambda_q2, da_lambda_k2,
           da_subln, cmp_pe_k, cmp_w1_k, cmp_w2_k, cmp_pe_v, cmp_w1_v, cmp_w2_v, w_up, w_down):
    depth = w_in.shape[0]
    n_p, seq = x_prompt.shape[:2]
    nbatch, t = x_sample.shape[:2]
    n_pool = cache_da_k.shape[1]
    past = page_table.shape[1] * PAGE
    assert n_p == 1 and cache_da_k.shape[2] == PAGE
    xp = x_prompt.reshape(seq, D_MODEL)
    xs = x_sample.reshape(nbatch * t, D_MODEL)
    pos_p = jnp.arange(seq)
    pos_s = past + jnp.tile(jnp.arange(t), nbatch)
    w_keep = min(WINDOW, seq)
    p_rows, s_rows = [], []
    for layer in range(depth):
        lam_init = 0.8 - 0.6 * math.exp(-0.3 * layer)
        lams = [a[layer].reshape(1, DA_DIM).astype(F32)
                for a in (da_lambda_q1, da_lambda_k1, da_lambda_q2, da_lambda_k2)]
        subln = da_subln[layer].reshape(1, LANES).astype(F32)
        wp = _prep_w_in(w_in[layer])
        cwk = _prep_cmp(cmp_pe_k[layer], cmp_w1_k[layer], cmp_w2_k[layer])
        cwv = _prep_cmp(cmp_pe_v[layer], cmp_w1_v[layer], cmp_w2_v[layer])
        wa, wb = _prep_w_out(w_out[layer])
        wu, wd = w_up[layer].astype(BF16), w_down[layer].astype(BF16)
        post = (norm_mix_post[layer], norm_ffn_pre[layer], norm_ffn_post[layer], wa, wb, wu, wd)

        p = _project(xp, pos_p, wp, norm_mix_pre[layer], BF16, 512)
        pt = _project_t(xp, pos_p, _prep_w_in_t(w_in[layer]), norm_mix_pre[layer], 512)
        oa_p = _da_prompt(pt, p["ka_b"], lams, subln, lam_init, 256, 512, 2)
        ck = _compress_prompt(p["kc"], cwk)
        cv = _compress_prompt(p["vc"], cwv)
        ob_p = _nsa_prompt(p, ck, cv, 128, 512)
        kv_t = lambda a: jnp.transpose(a.reshape(NSA_GROUPS, NSA_DIM, -1), (2, 0, 1))[None]
        p_rows.append((jnp.transpose(pt["kat"].reshape(DA_HEADS, 2, DA_DIM, seq), (3, 0, 1, 2))[None],
                       p["va"].reshape(1, seq, DA_HEADS, 2 * DA_DIM),
                       kv_t(pt["kct"]), kv_t(pt["vct"]), kv_t(pt["kst"]), kv_t(pt["vst"]),
                       kv_t(pt["kwt"][:, seq - w_keep:]), kv_t(pt["vwt"][:, seq - w_keep:])))

        s = _project(xs, pos_s, wp, norm_mix_pre[layer], F32, 512)
        da_k_t = jnp.transpose(cache_da_k[layer], (0, 2, 3, 4, 1)).reshape(n_pool, DA_OUT, PAGE)
        da_v_r = cache_da_v[layer].reshape(n_pool, PAGE * DA_HEADS, 2 * DA_DIM)
        oa_s = _da_sample(s, page_table, da_k_t, da_v_r, lams, subln, lam_init, t)
        pk = lambda c: c[layer].reshape(n_pool, PAGE, KV_W)
        pk_t = lambda c: jnp.transpose(c[layer], (0, 2, 3, 1)).reshape(-1, KV_W, c.shape[2])
        cks, cvs = _compress_sample(page_table, pk(cache_nsa_cmp_k), pk(cache_nsa_cmp_v), cwk, cwv)
        ob_s = _nsa_sample(s, cks, cvs, page_table, pk_t(cache_nsa_sel_k), pk_t(cache_nsa_sel_v),
                           pk_t(state_nsa_win_k), pk_t(state_nsa_win_v), past, t)
        kvs = lambda a: a.reshape(nbatch, t, NSA_GROUPS, NSA_DIM)
        nwk = jnp.concatenate([state_nsa_win_k[layer], kvs(s["kw"])], axis=1)[:, t:]
        nwv = jnp.concatenate([state_nsa_win_v[layer], kvs(s["vw"])], axis=1)[:, t:]
        s_rows.append((s["ka"].reshape(nbatch, t, DA_HEADS, 2, DA_DIM), s["va"].reshape(nbatch, t, DA_HEADS, 2 * DA_DIM),
                       kvs(s["kc"]), kvs(s["vc"]), kvs(s["ks"]), kvs(s["vs"]), nwk, nwv))

        xp = _finish(xp, oa_p, ob_p, *post, 256)
        xs = _finish(xs, oa_s, ob_s, *post, 256)

    p_out = [jnp.stack(list(c), axis=0) for c in zip(*p_rows)]
    s_out = [jnp.stack(list(c), axis=0) for c in zip(*s_rows)]
    return (xp.reshape(1, seq, D_MODEL), xs.reshape(nbatch, t, D_MODEL), *p_out, *s_out)
```

```python
import functools
import math

import jax
import jax.numpy as jnp
import numpy as np
from jax import lax
from jax.experimental import pallas as pl
from jax.experimental.pallas import tpu as pltpu

F32 = jnp.float32
BF16 = jnp.bfloat16

LANES = 128
D_MODEL = 1024
DA_HEADS = 4
DA_DIM = 64
DA_OUT = DA_HEADS * 2 * DA_DIM
NSA_HEADS = 8
NSA_GROUPS = 2
NSA_HPG = NSA_HEADS // NSA_GROUPS
NSA_DIM = 64
NSA_OUT = NSA_HEADS * NSA_DIM
KV_W = NSA_GROUPS * NSA_DIM
CMP_STRIDE = 16
CMP_BLOCK = 32
CMP_HIDDEN = 128
SEL_BLOCK = 64
CMP_PER_SEL = SEL_BLOCK // CMP_STRIDE
SEL_TOPK = 16
WINDOW = 512
N_BRANCH = 3
ROPE_THETA = 500000.0
ROPE_DIMS = 16
D_FF = 4 * D_MODEL
EPS = 1e-6
NEG = -1e30
PAGE = 128
SCALE = 0.125
LOG2E = math.log2(math.e)
HALF_W = CMP_STRIDE * KV_W

VMEM_LIMIT = 56 * 1024 * 1024

C_QA, C_KA, C_VA, C_QB = 0, 512, 1024, 1536
C_KV = 2560
C_GT = C_KV + 6 * KV_W
N_PROJ = C_GT + LANES


def _dot(a, b):
    return jnp.dot(a, b, preferred_element_type=F32)


def _dot_nt(a, b):
    return lax.dot_general(a, b, (((1,), (1,)), ((), ())), preferred_element_type=F32)


def _params(sem=None):
    return pltpu.CompilerParams(dimension_semantics=sem, vmem_limit_bytes=VMEM_LIMIT)


def _vmem_spec():
    return pl.BlockSpec(memory_space=pltpu.VMEM)


def _row_tile(rows, preferred):
    return preferred if rows % preferred == 0 else rows


def _prep_w_in(w):
    qa, ka, va = w[:, 0:512], w[:, 512:1024], w[:, 1024:1536]
    qb = w[:, 1536:2048].reshape(D_MODEL, NSA_HEADS, NSA_DIM)
    kv6 = w[:, 2048:2816]
    gl = w[:, 2816:2840]
    z = jnp.zeros((D_MODEL, NSA_DIM), w.dtype)
    qb_w = []
    for h in range(NSA_HEADS):
        qb_w += [qb[:, h], z] if h // NSA_HPG == 0 else [z, qb[:, h]]
    gl_p = jnp.concatenate([gl, jnp.zeros((D_MODEL, LANES - gl.shape[1]), w.dtype)], axis=1)
    return jnp.concatenate([qa, ka, va] + qb_w + [kv6, gl_p], axis=1).astype(BF16)


def _prep_cmp(pe, w1, w2):
    def expand(w1h):
        w = w1h.reshape(CMP_STRIDE, NSA_DIM, CMP_HIDDEN)
        z = jnp.zeros_like(w)
        g0 = jnp.concatenate([w, z], axis=-1)
        g1 = jnp.concatenate([z, w], axis=-1)
        return jnp.stack([g0, g1], axis=1).reshape(HALF_W, 2 * CMP_HIDDEN).astype(BF16)

    def tile_pe(p):
        return jnp.tile(p[:, None, :], (1, NSA_GROUPS, 1)).reshape(1, HALF_W).astype(F32)

    n = CMP_STRIDE * NSA_DIM
    z2 = jnp.zeros_like(w2)
    w2e = jnp.concatenate([jnp.concatenate([w2, z2], axis=1),
                           jnp.concatenate([z2, w2], axis=1)], axis=0).astype(BF16)
    return (tile_pe(pe[:CMP_STRIDE]), tile_pe(pe[CMP_STRIDE:]), expand(w1[:n]), expand(w1[n:]), w2e)


def _prep_w_out(w_out):
    wa = w_out[:DA_OUT]
    wb = w_out[DA_OUT:].reshape(NSA_HEADS, NSA_DIM, D_MODEL)
    order = [0, 4, 1, 5, 2, 6, 3, 7]
    wb = jnp.concatenate([wb[h] for h in order], axis=0)
    return wa.astype(BF16), wb.astype(BF16)


def _rope_tables(pos):
    half = ROPE_DIMS // 2
    inv = jnp.power(jnp.float32(ROPE_THETA), -jnp.arange(half, dtype=F32) * (2.0 / ROPE_DIMS))
    ang = pos.astype(F32)[:, None] * inv[None, :]
    cos, sin = jnp.cos(ang), jnp.sin(ang)
    n = pos.shape[0]
    one = jnp.ones((n, NSA_DIM - ROPE_DIMS), F32)
    zero = jnp.zeros((n, NSA_DIM - ROPE_DIMS), F32)
    z8 = jnp.zeros((n, half), F32)
    c = jnp.concatenate([cos, cos, one], axis=1)
    a = jnp.concatenate([-sin, z8, zero], axis=1)
    b = jnp.concatenate([z8, sin, zero], axis=1)
    return tuple(jnp.concatenate([t, t], axis=1) for t in (c, a, b))


def _expand_matrix_t(n_keys, n_blocks):
    k = np.arange(n_keys)[:, None] // SEL_BLOCK
    b = np.arange(n_blocks)[None, :]
    return jnp.asarray((k == b).astype(np.float32), dtype=BF16)


def _proj_kernel(x_ref, c_ref, a_ref, b_ref, g_ref, w_ref,
                 qa0_ref, qa1_ref, qb_ref, ka_ref, va_ref,
                 kc_ref, vc_ref, ks_ref, vs_ref, kw_ref, vw_ref, gt_ref,
                 kab_ref, vab_ref, ksb_ref, vsb_ref, kwb_ref, vwb_ref):
    x = x_ref[...]
    ms = jnp.mean(x * x, axis=-1, keepdims=True)
    h = (x * lax.rsqrt(ms + EPS) * g_ref[...]).astype(BF16)
    c, a, b = c_ref[...], a_ref[...], b_ref[...]

    def rope(z):
        return z * c + pltpu.roll(z, LANES - ROPE_DIMS // 2, 1) * a + pltpu.roll(z, ROPE_DIMS // 2, 1) * b

    def sect(col, width):
        return _dot(h, w_ref[:, col:col + width])

    lane = lax.broadcasted_iota(jnp.int32, (x.shape[0], LANES), 1)
    lo = lane < DA_DIM

    z = sect(C_QA, DA_OUT)
    for i in range(DA_HEADS):
        q = rope(z[:, i * LANES:(i + 1) * LANES]) * SCALE
        qa0_ref[:, i * LANES:(i + 1) * LANES] = jnp.where(lo, q, 0.0).astype(qa0_ref.dtype)
        qa1_ref[:, i * LANES:(i + 1) * LANES] = jnp.where(lo, 0.0, q).astype(qa1_ref.dtype)
    z = sect(C_KA, DA_OUT)
    for i in range(DA_HEADS):
        k = rope(z[:, i * LANES:(i + 1) * LANES])
        ka_ref[:, i * LANES:(i + 1) * LANES] = k
        kab_ref[:, i * LANES:(i + 1) * LANES] = k.astype(BF16)
    z = sect(C_VA, DA_OUT)
    va_ref[...] = z
    vab_ref[...] = z.astype(BF16)
    z = sect(C_QB, NSA_HEADS * LANES)
    for i in range(NSA_HEADS):
        qb_ref[:, i * LANES:(i + 1) * LANES] = (rope(z[:, i * LANES:(i + 1) * LANES]) * SCALE).astype(qb_ref.dtype)
    z = sect(C_KV, 6 * KV_W)
    kc_ref[...] = rope(z[:, 0:KV_W])
    vc_ref[...] = z[:, KV_W:2 * KV_W]
    k = rope(z[:, 2 * KV_W:3 * KV_W])
    ks_ref[...] = k
    ksb_ref[...] = k.astype(BF16)
    v = z[:, 3 * KV_W:4 * KV_W]
    vs_ref[...] = v
    vsb_ref[...] = v.astype(BF16)
    k = rope(z[:, 4 * KV_W:5 * KV_W])
    kw_ref[...] = k
    kwb_ref[...] = k.astype(BF16)
    v = z[:, 5 * KV_W:6 * KV_W]
    vw_ref[...] = v
    vwb_ref[...] = v.astype(BF16)
    gl = sect(C_GT, LANES)
    gt_ref[...] = 1.0 / (1.0 + jnp.exp(-gl))


def _project(x2d, pos, wp, g_pre, q_dtype, tm):
    rows = x2d.shape[0]
    tm = _row_tile(rows, tm)
    tabs = _rope_tables(pos)
    row_blk = lambda w: pl.BlockSpec((tm, w), lambda i: (i, 0))
    full = lambda shp: pl.BlockSpec(shp, lambda i: (0, 0))
    widths_f32 = [DA_OUT, DA_OUT] + [KV_W] * 6 + [LANES]
    widths_b16 = [DA_OUT, DA_OUT] + [KV_W] * 4
    out_shape = ([jax.ShapeDtypeStruct((rows, DA_OUT), q_dtype)] * 2
                 + [jax.ShapeDtypeStruct((rows, NSA_HEADS * LANES), q_dtype)]
                 + [jax.ShapeDtypeStruct((rows, w), F32) for w in widths_f32]
                 + [jax.ShapeDtypeStruct((rows, w), BF16) for w in widths_b16])
    out_specs = ([row_blk(DA_OUT)] * 2 + [row_blk(NSA_HEADS * LANES)]
                 + [row_blk(w) for w in widths_f32] + [row_blk(w) for w in widths_b16])
    outs = pl.pallas_call(
        _proj_kernel,
        grid=(rows // tm,),
        in_specs=[row_blk(D_MODEL), row_blk(LANES), row_blk(LANES), row_blk(LANES),
                  full((1, D_MODEL)), full((D_MODEL, N_PROJ))],
        out_specs=out_specs,
        out_shape=out_shape,
        compiler_params=_params(("arbitrary",)),
        name="proj",
    )(x2d, *tabs, g_pre.reshape(1, D_MODEL).astype(F32), wp)
    names = ("qa0", "qa1", "qb", "ka", "va", "kc", "vc", "ks", "vs", "kw", "vw", "gt",
             "ka_b", "va_b", "ks_b", "vs_b", "kw_b", "vw_b")
    return dict(zip(names, outs))


R_QA, R_QB, R_KA, R_VA, R_KV = 0, 512, 1536, 2048, 2560
R_GT = R_KV + 6 * KV_W
N_GATES = NSA_HEADS * N_BRANCH
N_PROJ_T = R_GT + N_GATES


def _prep_w_in_t(w):
    wp = _prep_w_in(w)
    return jnp.concatenate([wp[:, C_QA:C_KA], wp[:, C_QB:C_KV], wp[:, C_KA:C_QB],
                            wp[:, C_KV:C_GT + N_GATES]], axis=1).T


def _rope_tables_t(pos):
    return tuple(tab.T for tab in _rope_tables(pos))


def _proj_t_kernel(x_ref, c_ref, a_ref, b_ref, g_ref, wt_ref,
                   qa0t_ref, qa1t_ref, qbt_ref, vat_b_ref, vst_b_ref, vwt_b_ref,
                   kat_ref, kct_ref, vct_ref, kst_ref, vst_ref, kwt_ref, vwt_ref, gtt_ref):
    x = x_ref[...]
    ms = jnp.mean(x * x, axis=-1, keepdims=True)
    h = (x * lax.rsqrt(ms + EPS) * g_ref[...]).astype(BF16)
    c, a, b = c_ref[...], a_ref[...], b_ref[...]

    def rope(z):
        return z * c + pltpu.roll(z, LANES - ROPE_DIMS // 2, 0) * a + pltpu.roll(z, ROPE_DIMS // 2, 0) * b

    def sect(r, n):
        return _dot_nt(wt_ref[r:r + n, :], h)

    def grp(z, i):
        return z[i * LANES:(i + 1) * LANES]

    lo = lax.broadcasted_iota(jnp.int32, (LANES, x.shape[0]), 0) < DA_DIM
    z = sect(R_QA, DA_OUT)
    for i in range(DA_HEADS):
        q = rope(grp(z, i)) * (SCALE * LOG2E)
        qa0t_ref[i * LANES:(i + 1) * LANES, :] = jnp.where(lo, q, 0.0).astype(BF16)
        qa1t_ref[i * LANES:(i + 1) * LANES, :] = jnp.where(lo, 0.0, q).astype(BF16)
    z = sect(R_QB, NSA_HEADS * LANES)
    for i in range(NSA_HEADS):
        qbt_ref[i * LANES:(i + 1) * LANES, :] = (rope(grp(z, i)) * (SCALE * LOG2E)).astype(BF16)
    z = sect(R_KA, DA_OUT)
    for i in range(DA_HEADS):
        kat_ref[i * LANES:(i + 1) * LANES, :] = rope(grp(z, i))
    vat_b_ref[...] = sect(R_VA, DA_OUT).astype(BF16)
    z = sect(R_KV, 6 * KV_W)
    kct_ref[...] = rope(grp(z, 0))
    vct_ref[...] = grp(z, 1)
    kst_ref[...] = rope(grp(z, 2))
    vst_ref[...] = grp(z, 3)
    vst_b_ref[...] = grp(z, 3).astype(BF16)
    kwt_ref[...] = rope(grp(z, 4))
    vwt_ref[...] = grp(z, 5)
    vwt_b_ref[...] = grp(z, 5).astype(BF16)
    gtt_ref[...] = 1.0 / (1.0 + jnp.exp(-sect(R_GT, N_GATES)))


def _project_t(x2d, pos, wpt, g_pre, tm):
    rows = x2d.shape[0]
    tm = _row_tile(rows, tm)
    tabs = _rope_tables_t(pos)
    col_blk = lambda n: pl.BlockSpec((n, tm), lambda i: (0, i))
    full = lambda shp: pl.BlockSpec(shp, lambda i: (0, 0))
    heights = ([(DA_OUT, BF16)] * 2 + [(NSA_HEADS * LANES, BF16), (DA_OUT, BF16), (KV_W, BF16), (KV_W, BF16),
                                      (DA_OUT, F32)] + [(KV_W, F32)] * 6 + [(N_GATES, F32)])
    outs = pl.pallas_call(
        _proj_t_kernel,
        grid=(rows // tm,),
        in_specs=[pl.BlockSpec((tm, D_MODEL), lambda i: (i, 0)), col_blk(LANES), col_blk(LANES), col_blk(LANES),
                  full((1, D_MODEL)), full((N_PROJ_T, D_MODEL))],
        out_specs=[col_blk(n) for n, _ in heights],
        out_shape=[jax.ShapeDtypeStruct((n, rows), dt) for n, dt in heights],
        compiler_params=_params(("arbitrary",)),
        name="proj_t",
    )(x2d, *tabs, g_pre.reshape(1, D_MODEL).astype(F32), wpt)
    names = ("qa0t", "qa1t", "qbt", "vat_b", "vst_b", "vwt_b", "kat", "kct", "vct", "kst", "vst", "kwt", "vwt", "gtt")
    return dict(zip(names, outs))


def _compress_halves(rows, pet, peb, w1t, w1b, w2):
    nh = rows.shape[0]
    top = _dot((rows + pet).astype(BF16), w1t)
    bot = _dot((rows + peb).astype(BF16), w1b)
    hid = top + pltpu.roll(bot, nh - 1, 0)
    act = hid * (1.0 / (1.0 + jnp.exp(-hid)))
    return _dot(act.astype(BF16), w2)


def _store_permuted(tok, tmp_ref, out_ref):
    nb = tok.shape[0] // CMP_PER_SEL
    tmp_ref[...] = tok
    for r in range(CMP_PER_SEL):
        out_ref[r * nb:(r + 1) * nb, :] = tmp_ref[pl.ds(r, nb, stride=CMP_PER_SEL), :].astype(out_ref.dtype)


def _cmp_prompt_kernel(rows_ref, pet_ref, peb_ref, w1t_ref, w1b_ref, w2_ref, out_ref, tmp_ref):
    tok = _compress_halves(rows_ref[...], pet_ref[...], peb_ref[...], w1t_ref[...], w1b_ref[...], w2_ref[...])
    _store_permuted(tok, tmp_ref, out_ref)


def _compress_prompt(rows, cw):
    nh = rows.shape[0] // CMP_STRIDE
    return pl.pallas_call(
        _cmp_prompt_kernel,
        in_specs=[_vmem_spec()] * 6,
        out_specs=_vmem_spec(),
        out_shape=jax.ShapeDtypeStruct((nh, KV_W), BF16),
        scratch_shapes=[pltpu.VMEM((nh, KV_W), F32)],
        compiler_params=_params(),
        name="cmp_prompt",
    )(rows.reshape(nh, HALF_W), *cw)


def _lambda(lq1, lk1, lq2, lk2, lam_init):
    return (jnp.exp(jnp.sum(lq1 * lk1, axis=-1, keepdims=True))
            - jnp.exp(jnp.sum(lq2 * lk2, axis=-1, keepdims=True)) + lam_init)


def _subln(o, g, lam_init):
    ms = jnp.mean(o * o, axis=-1, keepdims=True)
    return o * lax.rsqrt(ms + EPS) * g * (1.0 - lam_init)


ONES_ROWS = 16


def _da_prompt_kernel(lq1_ref, lk1_ref, lq2_ref, lk2_ref, sg_ref, q0t_ref, q1t_ref, k_ref, vt_ref, o_ref,
                      *scratch, tq, tk, nh, lam_init):
    m_scs, acc_scs = scratch[:nh], scratch[nh:]
    qi = pl.program_id(1)
    q0 = qi * tq
    lam = _lambda(lq1_ref[...], lk1_ref[...], lq2_ref[...], lk2_ref[...], lam_init)
    qcats = [jnp.concatenate([q0t_ref[hd * LANES:(hd + 1) * LANES, :], q1t_ref[hd * LANES:(hd + 1) * LANES, :]],
                             axis=1) for hd in range(nh)]
    for hd in range(nh):
        m_scs[hd][...] = jnp.full(m_scs[hd].shape, NEG, F32)
        acc_scs[hd][...] = jnp.zeros(acc_scs[hd].shape, F32)

    def tile(start, n, masked):
        items = [(sb, hd) for sb in range(n // tq) for hd in range(nh)]

        def at(i):
            sb, hd = items[i]
            return pl.multiple_of(start + sb * tq, tq), hd

        def score(i):
            s0, hd = at(i)
            return _dot(k_ref[pl.ds(s0, tq), hd * LANES:(hd + 1) * LANES], qcats[hd])

        def probs(i, st):
            s0, hd = at(i)
            if masked:
                kpos = s0 + lax.broadcasted_iota(jnp.int32, st.shape, 0)
                qpos = q0 + (lax.broadcasted_iota(jnp.int32, st.shape, 1) & (tq - 1))
                st = jnp.where(kpos <= qpos, st, NEG)
            m_old = m_scs[hd][...]
            m_new = jnp.maximum(m_old, jnp.max(st, axis=0, keepdims=True))
            m_scs[hd][...] = m_new
            return jnp.exp2(st - m_new[0:1]).astype(BF16), jnp.exp2(m_old - m_new)

        def pv(i, pt):
            s0, hd = at(i)
            vt = jnp.concatenate([vt_ref[hd * LANES:(hd + 1) * LANES, pl.ds(s0, tq)],
                                  jnp.ones((ONES_ROWS, tq), BF16)], axis=0)
            return _dot(vt, pt)

        def finish(i, alpha, o):
            hd = items[i][1]
            acc_scs[hd][...] = acc_scs[hd][...] * alpha[0:1] + o

        _staggered(len(items), score, probs, pv, finish)

    def body(kj, carry):
        tile(pl.multiple_of(kj * tk, tk), tk, False)
        return carry

    n_big = q0 // tk
    lax.fori_loop(0, n_big, body, 0)
    for d in range(tk // tq - 1):
        @pl.when(q0 - n_big * tk > d * tq)
        def _():
            tile(pl.multiple_of(n_big * tk + d * tq, tq), tq, False)
    tile(pl.multiple_of(q0, tq), tq, True)
    for hd in range(nh):
        a0, a1 = acc_scs[hd][:, :tq], acc_scs[hd][:, tq:]
        ot = a0[:LANES] / a0[LANES:LANES + 1] - lam * (a1[:LANES] / a1[LANES:LANES + 1])
        ms = jnp.mean(ot * ot, axis=0, keepdims=True)
        yt = ot * lax.rsqrt(ms + EPS) * sg_ref[...] * (1.0 - lam_init)
        o_ref[:, hd * LANES:(hd + 1) * LANES] = yt.T


def _da_prompt(pt, ka_b, lams, subln, lam_init, tq, tk, nh):
    seq = ka_b.shape[0]
    assert seq % tq == 0 and tk % tq == 0 and tq & (tq - 1) == 0 and DA_HEADS % nh == 0
    w = nh * LANES
    vec = lambda: pl.BlockSpec((1, DA_DIM), lambda h, i: (0, 0))
    qblk = pl.BlockSpec((w, tq), lambda h, i: (h, i))
    sg = jnp.broadcast_to(subln.reshape(LANES, 1), (LANES, tq))
    return pl.pallas_call(
        functools.partial(_da_prompt_kernel, tq=tq, tk=tk, nh=nh, lam_init=lam_init),
        grid=(DA_HEADS // nh, seq // tq),
        in_specs=[vec(), vec(), vec(), vec(), pl.BlockSpec((LANES, tq), lambda h, i: (0, 0)),
                  qblk, qblk, pl.BlockSpec((seq, w), lambda h, i: (0, h)),
                  pl.BlockSpec((w, seq), lambda h, i: (h, 0))],
        out_specs=pl.BlockSpec((tq, w), lambda h, i: (i, h)),
        out_shape=jax.ShapeDtypeStruct((seq, DA_OUT), F32),
        scratch_shapes=([pltpu.VMEM((8, 2 * tq), F32)] * nh
                        + [pltpu.VMEM((LANES + ONES_ROWS, 2 * tq), F32)] * nh),
        compiler_params=_params(("arbitrary", "arbitrary")),
        name="da_prompt",
    )(*lams, sg, pt["qa0t"], pt["qa1t"], ka_b, pt["vat_b"])


def _softmax_rows(s, mask):
    s = jnp.where(mask, s, NEG)
    m = jnp.max(s, axis=-1, keepdims=True)
    e = jnp.where(mask, jnp.exp(s - m), 0.0)
    l = jnp.sum(e, axis=-1, keepdims=True)
    return e * (1.0 / jnp.maximum(l, 1e-30))


def _select_blocks(imp, qblk):
    nb = imp.shape[1]
    jb = lax.broadcasted_iota(jnp.int32, imp.shape, 1)
    jbf = jb.astype(F32)
    forced = (jb == 0) | (jb == qblk) | (jb == qblk - 1)
    valid = jb <= qblk
    score = jnp.where(forced, 1e30, jnp.where(valid, imp, -1.0))
    picked = jnp.zeros(imp.shape, F32)
    for _ in range(min(SEL_TOPK, nb)):
        mx = jnp.max(score, axis=-1, keepdims=True)
        first = jnp.min(jnp.where(score == mx, jbf, 1e9), axis=-1, keepdims=True)
        hit = jbf == first
        picked = jnp.where(hit, 1.0, picked)
        score = jnp.where(hit, -2.0, score)
    return jnp.where(valid, picked, 0.0)


def _softmax_cols(st, mask):
    st = jnp.where(mask, st, NEG)
    m = jnp.max(st, axis=0, keepdims=True)
    e = jnp.where(mask, jnp.exp2(st - m), 0.0)
    l = jnp.sum(e, axis=0, keepdims=True)
    return e * (1.0 / jnp.maximum(l, 1e-30))


def _select_blocks_t(imp_t, qblk):
    nb = imp_t.shape[0]
    jb = lax.broadcasted_iota(jnp.int32, imp_t.shape, 0)
    jbf = jb.astype(F32)
    forced = (jb == 0) | (jb == qblk) | (jb == qblk - 1)
    valid = jb <= qblk
    score = jnp.where(forced, 1e30, jnp.where(valid, imp_t, -1.0))
    picked = jnp.zeros(imp_t.shape, F32)
    for _ in range(min(SEL_TOPK, nb)):
        mx = jnp.max(score, axis=0, keepdims=True)
        first = jnp.min(jnp.where(score == mx, jbf, 1e9), axis=0, keepdims=True)
        hit = jbf == first
        picked = jnp.where(hit, 1.0, picked)
        score = jnp.where(hit, -2.0, score)
    return jnp.where(valid, picked, 0.0)


def _staggered(n, score, probs, pv, finish):
    s_next = score(0)
    pending = None
    for i in range(n):
        s_cur = s_next
        if i + 1 < n:
            s_next = score(i + 1)
        p, aux = probs(i, s_cur)
        out = pv(i, p)
        if pending is not None:
            finish(*pending)
        pending = (i, aux, out)
    finish(*pending)


def _nsa_prompt_t_kernel(cend_ref, qt_ref, gt_ref, ck_ref, cvt_ref, ks_ref, vst_ref, kw_ref, vwt_ref, o_ref,
                         sel_sc, res_sc, *scratch, tq, tk):
    m_scs, acc_scs = scratch[:NSA_HEADS], scratch[NSA_HEADS:]
    qi = pl.program_id(0)
    q0 = qi * tq
    nb = sel_sc.shape[1]
    qpos = q0 + lax.broadcasted_iota(jnp.int32, (1, tq), 1)

    def q_head(h):
        return qt_ref[h * LANES:(h + 1) * LANES, :]

    def gate(h, br):
        i = h * N_BRANCH + br
        return gt_ref[i:i + 1, :]

    def with_ones(vt):
        return jnp.concatenate([vt, jnp.ones((ONES_ROWS, vt.shape[1]), BF16)], axis=0)

    ck = ck_ref[...]
    cvt = cvt_ref[...]
    cmask = cend_ref[...] <= qpos
    psums = [None] * NSA_GROUPS

    def cmp_probs(h, st):
        p = _softmax_cols(st, cmask)
        g = h // NSA_HPG
        psums[g] = p if psums[g] is None else psums[g] + p
        return p.astype(BF16), None

    def cmp_finish(h, _, o):
        res_sc[h] = o * gate(h, 0)

    _staggered(NSA_HEADS, lambda h: _dot(ck, q_head(h)), cmp_probs, lambda h, p: _dot(cvt, p), cmp_finish)
    imps = []
    for g in range(NSA_GROUPS):
        imp = psums[g][0:nb]
        for r in range(1, CMP_PER_SEL):
            imp = imp + psums[g][r * nb:(r + 1) * nb]
        imps.append(imp)
    sel = _select_blocks_t(jnp.concatenate(imps, axis=1), jnp.concatenate([qpos, qpos], axis=1) // SEL_BLOCK)
    for g in range(NSA_GROUPS):
        sel_sc[g] = sel[:, g * tq:(g + 1) * tq]

    for h in range(NSA_HEADS):
        m_scs[h][...] = jnp.full(m_scs[h].shape, NEG, F32)
        acc_scs[h][...] = jnp.zeros(acc_scs[h].shape, F32)

    def tile(start, n, causal):
        ks = ks_ref[pl.ds(start, n), :]
        vst = with_ones(vst_ref[:, pl.ds(start, n)])
        b0 = start // SEL_BLOCK
        allowed = []
        for g in range(NSA_GROUPS):
            rows = [jnp.broadcast_to(sel_sc[g, pl.ds(b0 + i, 1), :], (SEL_BLOCK, tq))
                    for i in range(n // SEL_BLOCK)]
            ok = jnp.concatenate(rows, axis=0) > 0.5
            if causal:
                kpos = start + lax.broadcasted_iota(jnp.int32, (n, tq), 0)
                ok = ok & (kpos <= qpos)
            allowed.append(ok)

        def sel_probs(h, st):
            st = jnp.where(allowed[h // NSA_HPG], st, NEG)
            m_old = m_scs[h][...]
            m_new = jnp.maximum(m_old, jnp.max(st, axis=0, keepdims=True))
            m_scs[h][...] = m_new
            return jnp.exp2(st - m_new[0:1]).astype(BF16), jnp.exp2(m_old - m_new)

        def sel_finish(h, alpha, o):
            acc_scs[h][...] = acc_scs[h][...] * alpha[0:1] + o

        _staggered(NSA_HEADS, lambda h: _dot(ks, q_head(h)), sel_probs, lambda h, p: _dot(vst, p), sel_finish)

    def body(kj, carry):
        tile(pl.multiple_of(kj * tk, tk), tk, False)
        return carry

    n_big = q0 // tk
    lax.fori_loop(0, n_big, body, 0)
    for d in range(tk // tq - 1):
        @pl.when(q0 - n_big * tk > d * tq)
        def _():
            tile(pl.multiple_of(n_big * tk + d * tq, tq), tq, False)
    tile(pl.multiple_of(q0, tq), tq, True)

    wlen = WINDOW + tq
    wstart = pl.multiple_of(jnp.maximum(q0 - WINDOW, 0), tq)
    kw = kw_ref[pl.ds(wstart, wlen), :]
    vwt = with_ones(vwt_ref[:, pl.ds(wstart, wlen)])
    kpos = wstart + lax.broadcasted_iota(jnp.int32, (wlen, tq), 0)
    wmask = (kpos <= qpos) & (kpos > qpos - WINDOW)
    def win_probs(h, sw):
        sw = jnp.where(wmask, sw, NEG)
        return jnp.exp2(sw - jnp.max(sw, axis=0, keepdims=True)).astype(BF16), None

    def win_finish(h, _, ow):
        acc = acc_scs[h][...]
        res_sc[h] = (res_sc[h] + (acc[:LANES] / acc[LANES:LANES + 1]) * gate(h, 1)
                     + (ow[:LANES] / ow[LANES:LANES + 1]) * gate(h, 2))

    _staggered(NSA_HEADS, lambda h: _dot(kw, q_head(h)), win_probs, lambda h, p: _dot(vwt, p), win_finish)
    lo = lax.broadcasted_iota(jnp.int32, (LANES, tq), 0) < NSA_DIM
    for h in range(NSA_HPG):
        o_ref[:, h * LANES:(h + 1) * LANES] = jnp.where(lo, res_sc[h], res_sc[h + NSA_HPG]).T


def _nsa_prompt_t(pt, ks_b, kw_b, ck, cv, tq, tk):
    seq = ks_b.shape[0]
    nc = seq // CMP_STRIDE
    nb = seq // SEL_BLOCK
    assert seq % tq == 0 and tk % tq == 0 and WINDOW % tq == 0 and seq >= WINDOW + tq and tq % SEL_BLOCK == 0
    pp = np.arange(nc)
    cend = ((CMP_PER_SEL * (pp % nb) + pp // nb) * CMP_STRIDE + (CMP_BLOCK - 1)).astype(np.int32)
    cend = jnp.asarray(np.broadcast_to(cend[:, None], (nc, tq)))
    return pl.pallas_call(
        functools.partial(_nsa_prompt_t_kernel, tq=tq, tk=tk),
        grid=(seq // tq,),
        in_specs=[_vmem_spec(),
                  pl.BlockSpec((NSA_HEADS * LANES, tq), lambda i: (0, i)),
                  pl.BlockSpec((N_GATES, tq), lambda i: (0, i))] + [_vmem_spec()] * 6,
        out_specs=pl.BlockSpec((tq, NSA_OUT), lambda i: (i, 0)),
        out_shape=jax.ShapeDtypeStruct((seq, NSA_OUT), F32),
        scratch_shapes=([pltpu.VMEM((NSA_GROUPS, nb, tq), F32), pltpu.VMEM((NSA_HEADS, LANES, tq), F32)]
                        + [pltpu.VMEM((8, tq), F32)] * NSA_HEADS
                        + [pltpu.VMEM((LANES + ONES_ROWS, tq), F32)] * NSA_HEADS),
        compiler_params=_params(("arbitrary",)),
        name="nsa_prompt",
    )(cend, pt["qbt"], pt["gtt"], ck, cv.T, ks_b, pt["vst_b"], kw_b, pt["vwt_b"])


def _page_copies(pt_ref, seq, first_page, n_pages, slot, streams, sems):
    out = []
    for j in range(n_pages):
        pg = pt_ref[seq, first_page + j]
        for i, (cache, dst) in enumerate(streams):
            out.append(pltpu.make_async_copy(cache.at[pg], dst(slot, j), sems.at[i, slot]))
    return out


def _rows_dst(buf, rows):
    return lambda slot, j: buf.at[slot, pl.ds(j * rows, rows)]


def _cols_dst(buf, cols):
    return lambda slot, j: buf.at[slot, :, pl.ds(j * cols, cols)]


def _online_update(state, s, pv):
    m_old, l_old, acc = state
    m_new = jnp.maximum(m_old, jnp.max(s, axis=-1, keepdims=True))
    alpha = jnp.exp(m_old - m_new)
    p = jnp.exp(s - m_new)
    l_new = alpha * l_old + jnp.sum(p, axis=-1, keepdims=True)
    return m_new, l_new, alpha * acc + pv(p.astype(BF16))


def _pad_rows(x, rows):
    return jnp.concatenate([x, jnp.zeros((rows - x.shape[0], x.shape[1]), x.dtype)], axis=0)


def _da_sample_kernel(pt_ref, lq1_ref, lk1_ref, lq2_ref, lk2_ref, sg_ref, q0_ref, q1_ref, kn_ref, vn_ref,
                      kc_hbm, vc_hbm, o_ref, kbuf, vbuf, sems, *, ch, nch, lam_init):
    b = pl.program_id(0)
    nbatch = pl.num_programs(0)
    t = q0_ref.shape[0]
    lam = _lambda(lq1_ref[...], lk1_ref[...], lq2_ref[...], lk2_ref[...], lam_init)

    streams = ((kc_hbm, _cols_dst(kbuf, PAGE)), (vc_hbm, _rows_dst(vbuf, PAGE * DA_HEADS)))

    def copies(seq, c, slot):
        return _page_copies(pt_ref, seq, c * ch, ch, slot, streams, sems)

    @pl.when(b == 0)
    def _():
        for cp in copies(0, 0, 0):
            cp.start()

    lane = lax.broadcasted_iota(jnp.int32, (t, DA_OUT), 1)
    parts = []
    for h in range(DA_HEADS):
        in_head = (lane >= h * LANES) & (lane < (h + 1) * LANES)
        parts += [jnp.where(in_head, q0_ref[...], 0.0), jnp.where(in_head, q1_ref[...], 0.0)]
    qbd = jnp.concatenate(parts, axis=0).astype(BF16)
    nrow = qbd.shape[0]
    hrows = 2 * t

    state = (jnp.full((nrow, 1), NEG, F32), jnp.zeros((nrow, 1), F32), jnp.zeros((nrow, LANES), F32))
    for c in range(nch):
        slot = c % 2
        if c + 1 < nch:
            for cp in copies(b, c + 1, (c + 1) % 2):
                cp.start()
        else:
            @pl.when(b + 1 < nbatch)
            def _():
                for cp in copies(b + 1, 0, 0):
                    cp.start()
        for cp in copies(b, c, slot):
            cp.wait()

        def pv(p, slot=slot):
            return jnp.concatenate(
                [_dot(p[h * hrows:(h + 1) * hrows],
                      vbuf[slot, pl.ds(h, ch * PAGE, stride=DA_HEADS), :].astype(BF16))
                 for h in range(DA_HEADS)], axis=0)

        state = _online_update(state, _dot(qbd, kbuf[slot].astype(BF16)), pv)

    kn = _pad_rows(kn_ref[...], LANES).astype(BF16)
    vn = _pad_rows(vn_ref[...], LANES).astype(BF16)
    col = lax.broadcasted_iota(jnp.int32, (nrow, LANES), 1)
    rowq = lax.broadcasted_iota(jnp.int32, (nrow, LANES), 0) & (t - 1)
    s_n = jnp.where((col < t) & (col <= rowq), _dot_nt(qbd, kn), NEG)

    def pv_new(p):
        full = _dot(p, vn)
        return jnp.concatenate([full[h * hrows:(h + 1) * hrows, h * LANES:(h + 1) * LANES]
                                for h in range(DA_HEADS)], axis=0)

    _, l, acc = _online_update(state, s_n, pv_new)
    out = acc / l
    for h in range(DA_HEADS):
        r = h * hrows
        o = out[r:r + t] - lam * out[r + t:r + 2 * t]
        o_ref[:, h * LANES:(h + 1) * LANES] = _subln(o, sg_ref[...], lam_init)


def _da_sample(s, page_table, cache_k, cache_v, lams, subln, lam_init, t):
    nbatch, n_pages = page_table.shape
    assert t == 8 and n_pages % 2 == 0
    ch = min(16, n_pages // 2)
    nch = n_pages // ch
    assert nch % 2 == 0 and nch * ch == n_pages
    vec = lambda w: pl.BlockSpec((1, w), lambda b, pt: (0, 0))
    blk = pl.BlockSpec((t, DA_OUT), lambda b, pt: (b, 0))
    anyspec = pl.BlockSpec(memory_space=pl.ANY)
    grid_spec = pltpu.PrefetchScalarGridSpec(
        num_scalar_prefetch=1,
        grid=(nbatch,),
        in_specs=[vec(DA_DIM)] * 4 + [vec(LANES), blk, blk, blk, blk, anyspec, anyspec],
        out_specs=blk,
        scratch_shapes=[pltpu.VMEM((2, DA_OUT, ch * PAGE), F32), pltpu.VMEM((2, ch * PAGE * DA_HEADS, LANES), F32),
                        pltpu.SemaphoreType.DMA((2, 2))],
    )
    return pl.pallas_call(
        functools.partial(_da_sample_kernel, ch=ch, nch=nch, lam_init=lam_init),
        grid_spec=grid_spec,
        out_shape=jax.ShapeDtypeStruct((nbatch * t, DA_OUT), F32),
        compiler_params=_params(("arbitrary",)),
        name="da_sample",
    )(page_table, *lams, subln, s["qa0"], s["qa1"], s["ka"], s["va"], cache_k, cache_v)


def _cmp_sample_kernel(pt_ref, petk_ref, pebk_ref, w1tk_ref, w1bk_ref, w2k_ref,
                       petv_ref, pebv_ref, w1tv_ref, w1bv_ref, w2v_ref, kc_hbm, vc_hbm,
                       ck_ref, cv_ref, kbuf, vbuf, rows_ref, tmp_ref, sems, *, n_pages):
    b = pl.program_id(0)
    nbatch = pl.num_programs(0)
    slot = b % 2
    page_dst = lambda buf: (lambda sl, j: buf.at[sl, j])
    streams = ((kc_hbm, page_dst(kbuf)), (vc_hbm, page_dst(vbuf)))

    def copies(seq, sl):
        return _page_copies(pt_ref, seq, 0, n_pages, sl, streams, sems)

    @pl.when(b == 0)
    def _():
        for cp in copies(0, 0):
            cp.start()

    @pl.when(b + 1 < nbatch)
    def _():
        for cp in copies(b + 1, 1 - slot):
            cp.start()

    for cp in copies(b, slot):
        cp.wait()

    def compress(buf, pet_ref, peb_ref, w1t_ref, w1b_ref, w2_ref, out_ref):
        group = 8 if n_pages % 8 == 0 else 1

        def to_rows(jg, carry):
            for u in range(group):
                j = jg * group + u
                rows_ref[pl.ds(pl.multiple_of(j * PAGE, PAGE), PAGE), :] = buf[slot, j].T
            return carry

        lax.fori_loop(0, n_pages // group, to_rows, 0)
        nh = rows_ref.shape[0] // CMP_STRIDE
        top = jnp.zeros((nh, 2 * CMP_HIDDEN), F32)
        bot = jnp.zeros((nh, 2 * CMP_HIDDEN), F32)
        for j in range(0, CMP_STRIDE, 2):
            x = jnp.concatenate([rows_ref[pl.ds(j, nh, stride=CMP_STRIDE), :],
                                 rows_ref[pl.ds(j + 1, nh, stride=CMP_STRIDE), :]], axis=1)
            c0, c1 = j * KV_W, (j + 2) * KV_W
            top = top + _dot((x + pet_ref[:, c0:c1]).astype(BF16), w1t_ref[c0:c1, :])
            bot = bot + _dot((x + peb_ref[:, c0:c1]).astype(BF16), w1b_ref[c0:c1, :])
        hid = top + pltpu.roll(bot, nh - 1, 0)
        act = hid * (1.0 / (1.0 + jnp.exp(-hid)))
        _store_permuted(_dot(act.astype(BF16), w2_ref[...]), tmp_ref, out_ref)

    compress(kbuf, petk_ref, pebk_ref, w1tk_ref, w1bk_ref, w2k_ref, ck_ref)
    compress(vbuf, petv_ref, pebv_ref, w1tv_ref, w1bv_ref, w2v_ref, cv_ref)


def _compress_sample(page_table, cache_k, cache_v, cwk, cwv):
    nbatch, n_pages = page_table.shape
    nh = n_pages * PAGE // CMP_STRIDE
    out_blk = pl.BlockSpec((None, nh, KV_W), lambda b, pt: (b, 0, 0))
    anyspec = pl.BlockSpec(memory_space=pl.ANY)
    grid_spec = pltpu.PrefetchScalarGridSpec(
        num_scalar_prefetch=1,
        grid=(nbatch,),
        in_specs=[_vmem_spec()] * 10 + [anyspec, anyspec],
        out_specs=[out_blk, out_blk],
        scratch_shapes=[pltpu.VMEM((2, n_pages, KV_W, PAGE), F32), pltpu.VMEM((2, n_pages, KV_W, PAGE), F32),
                        pltpu.VMEM((n_pages * PAGE, KV_W), F32), pltpu.VMEM((nh, KV_W), F32),
                        pltpu.SemaphoreType.DMA((2, 2))],
    )
    return pl.pallas_call(
        functools.partial(_cmp_sample_kernel, n_pages=n_pages),
        grid_spec=grid_spec,
        out_shape=[jax.ShapeDtypeStruct((nbatch, nh, KV_W), BF16)] * 2,
        compiler_params=_params(("arbitrary",)),
        name="cmp_sample",
    )(page_table, *cwk, *cwv, cache_k, cache_v)


def _nsa_sample_kernel(pt_ref, cend_ref, qb_ref, gt_ref, ck_ref, cv_ref, ksn_ref, vsn_ref, kwn_ref, vwn_ref,
                       wk_ref, wv_ref, e_ref, ks_hbm, vs_hbm, o_ref, kbuf, vbuf, sems, *, n_pages, past, chunk):
    b = pl.program_id(0)
    nbatch = pl.num_programs(0)
    slot = b % 2
    t = qb_ref.shape[0]
    nrow = NSA_HEADS * t
    nbp = past // SEL_BLOCK
    nbl = e_ref.shape[0]
    streams = ((ks_hbm, _cols_dst(kbuf, PAGE)), (vs_hbm, _cols_dst(vbuf, PAGE)))

    def copies(seq, sl):
        return _page_copies(pt_ref, seq, 0, n_pages, sl, streams, sems)

    @pl.when(b == 0)
    def _():
        for cp in copies(0, 0):
            cp.start()

    @pl.when(b + 1 < nbatch)
    def _():
        for cp in copies(b + 1, 1 - slot):
            cp.start()

    qn = jnp.concatenate([qb_ref[:, h * LANES:(h + 1) * LANES] for h in range(NSA_HEADS)], axis=0).astype(BF16)
    rowq = lax.broadcasted_iota(jnp.int32, (nrow, 1), 0) & (t - 1)
    qpos = past + rowq
    gt = gt_ref[...]

    def gate(br):
        return jnp.concatenate([gt[:, h * N_BRANCH + br:h * N_BRANCH + br + 1] for h in range(NSA_HEADS)], axis=0)

    p = _softmax_rows(_dot_nt(qn, ck_ref[...]), cend_ref[...] <= qpos)
    res = _dot(p.astype(BF16), cv_ref[...]) * gate(0)
    imps = []
    for g in range(NSA_GROUPS):
        psum = p[g * NSA_HPG * t:g * NSA_HPG * t + t]
        for hh in range(1, NSA_HPG):
            r = (g * NSA_HPG + hh) * t
            psum = psum + p[r:r + t]
        imp = psum[:, 0:nbp]
        for r in range(1, CMP_PER_SEL):
            imp = imp + psum[:, r * nbp:(r + 1) * nbp]
        imps.append(jnp.concatenate([imp, jnp.zeros((t, nbl - nbp), F32)], axis=1))
    q16 = past + (lax.broadcasted_iota(jnp.int32, (NSA_GROUPS * t, 1), 0) & (t - 1))
    sel = _select_blocks(jnp.concatenate(imps, axis=0), q16 // SEL_BLOCK)
    sel_rows = jnp.concatenate([sel[0:t]] * NSA_HPG + [sel[t:2 * t]] * NSA_HPG, axis=0)
    sel_b16 = sel_rows.astype(BF16)

    for cp in copies(b, slot):
        cp.wait()
    state = (jnp.full((nrow, 1), NEG, F32), jnp.zeros((nrow, 1), F32), jnp.zeros((nrow, KV_W), F32))
    for c in range(past // chunk):
        kst = kbuf[slot, :, pl.ds(c * chunk, chunk)].astype(BF16)
        vst = vbuf[slot, :, pl.ds(c * chunk, chunk)].astype(BF16)
        allowed = _dot(sel_b16, e_ref[:, c * chunk:(c + 1) * chunk]) > 0.5
        state = _online_update(state, jnp.where(allowed, _dot(qn, kst), NEG),
                               lambda p, vst=vst: _dot_nt(p, vst))
    col = lax.broadcasted_iota(jnp.int32, (nrow, LANES), 1)
    new_ok = (col < t) & (col <= rowq)
    s_n = _dot_nt(qn, _pad_rows(ksn_ref[...], LANES).astype(BF16))
    s_n = jnp.where(new_ok & (sel_rows[:, nbp:nbp + 1] > 0.5), s_n, NEG)
    vsn = _pad_rows(vsn_ref[...], LANES).astype(BF16)
    _, l, acc = _online_update(state, s_n, lambda p: _dot(p, vsn))
    res = res + (acc / l) * gate(1)

    wlen = wk_ref.shape[1]
    s_w = _dot(qn, wk_ref[...].astype(BF16))
    wcol = lax.broadcasted_iota(jnp.int32, (nrow, wlen), 1)
    s_wn = _dot_nt(qn, _pad_rows(kwn_ref[...], LANES).astype(BF16))
    pw = _softmax_rows(jnp.concatenate([s_w, s_wn], axis=1),
                       jnp.concatenate([wcol > rowq + (wlen - WINDOW), new_ok], axis=1))
    o_win = (_dot_nt(pw[:, :wlen].astype(BF16), wv_ref[...].astype(BF16))
             + _dot(pw[:, wlen:].astype(BF16), _pad_rows(vwn_ref[...], LANES).astype(BF16)))
    res = res + o_win * gate(2)

    lo = lax.broadcasted_iota(jnp.int32, (t, LANES), 1) < NSA_DIM
    for h in range(NSA_HPG):
        o_ref[:, h * LANES:(h + 1) * LANES] = jnp.where(lo, res[h * t:(h + 1) * t],
                                                        res[(h + NSA_HPG) * t:(h + NSA_HPG + 1) * t])


def _nsa_sample(s, ck, cv, page_table, cache_k, cache_v, win_k, win_v, past, t):
    nbatch, n_pages = page_table.shape
    nh = ck.shape[1]
    nbp = past // SEL_BLOCK
    nbl = -(-(nbp + 1) // LANES) * LANES
    wlen = win_k.shape[2]
    chunk = min(2048, past)
    assert t == 8 and past % chunk == 0 and past % SEL_BLOCK == 0 and wlen <= past and nh == CMP_PER_SEL * nbp
    pp = np.arange(nh)
    cend = ((CMP_PER_SEL * (pp % nbp) + pp // nbp) * CMP_STRIDE + (CMP_BLOCK - 1)).astype(np.int32)
    e = _expand_matrix_t(past, nbl).T
    row = lambda w: pl.BlockSpec((t, w), lambda b, pt: (b, 0))
    per_seq = lambda n, w: pl.BlockSpec((None, n, w), lambda b, pt: (b, 0, 0))
    anyspec = pl.BlockSpec(memory_space=pl.ANY)
    grid_spec = pltpu.PrefetchScalarGridSpec(
        num_scalar_prefetch=1,
        grid=(nbatch,),
        in_specs=[pl.BlockSpec((1, nh), lambda b, pt: (0, 0)), row(NSA_HEADS * LANES), row(LANES),
                  per_seq(nh, KV_W), per_seq(nh, KV_W), row(KV_W), row(KV_W), row(KV_W), row(KV_W),
                  per_seq(KV_W, wlen), per_seq(KV_W, wlen), _vmem_spec(), anyspec, anyspec],
        out_specs=row(NSA_OUT),
        scratch_shapes=[pltpu.VMEM((2, KV_W, past), F32), pltpu.VMEM((2, KV_W, past), F32),
                        pltpu.SemaphoreType.DMA((2, 2))],
    )
    return pl.pallas_call(
        functools.partial(_nsa_sample_kernel, n_pages=n_pages, past=past, chunk=chunk),
        grid_spec=grid_spec,
        out_shape=jax.ShapeDtypeStruct((nbatch * t, NSA_OUT), F32),
        compiler_params=_params(("arbitrary",)),
        name="nsa_sample",
    )(page_table, jnp.asarray(cend).reshape(1, nh), s["qb"], s["gt"], ck, cv, s["ks"], s["vs"], s["kw"], s["vw"],
      win_k, win_v, e, cache_k, cache_v)


def _rms(x, g):
    return x * lax.rsqrt(jnp.mean(x * x, axis=-1, keepdims=True) + EPS) * g


def _finish_kernel(x_ref, oa_ref, ob_ref, gp_ref, gf_ref, gq_ref, wa_ref, wb_ref, wu_ref, wd_ref, y_ref):
    mix = _dot(oa_ref[...].astype(BF16), wa_ref[...]) + _dot(ob_ref[...].astype(BF16), wb_ref[...])
    x1 = x_ref[...] + _rms(mix, gp_ref[...])
    h = _rms(x1, gf_ref[...]).astype(BF16)
    u = jnp.maximum(_dot(h, wu_ref[...]), 0.0)
    f = _dot((u * u).astype(BF16), wd_ref[...])
    y_ref[...] = x1 + _rms(f, gq_ref[...])


def _finish(x2d, oa, ob, g_post, g_ffn_pre, g_ffn_post, wa, wb, wu, wd, tm):
    rows = x2d.shape[0]
    tm = _row_tile(rows, tm)
    row = lambda w: pl.BlockSpec((tm, w), lambda i: (i, 0))
    vec = pl.BlockSpec((1, D_MODEL), lambda i: (0, 0))
    g = lambda a: a.reshape(1, D_MODEL).astype(F32)
    return pl.pallas_call(
        _finish_kernel,
        grid=(rows // tm,),
        in_specs=[row(D_MODEL), row(DA_OUT), row(NSA_OUT), vec, vec, vec] + [_vmem_spec()] * 4,
        out_specs=row(D_MODEL),
        out_shape=jax.ShapeDtypeStruct((rows, D_MODEL), F32),
        compiler_params=_params(("arbitrary",)),
        name="finish",
    )(x2d, oa, ob, g(g_post), g(g_ffn_pre), g(g_ffn_post), wa, wb, wu, wd)


def kernel(x_prompt, x_sample, cache_da_k, cache_da_v, cache_nsa_cmp_k, cache_nsa_cmp_v, cache_nsa_sel_k,
           cache_nsa_sel_v, state_nsa_win_k, state_nsa_win_v, page_table, norm_mix_pre, norm_mix_post,
           norm_ffn_pre, norm_ffn_post, w_in, w_out, da_lambda_q1, da_lambda_k1, da_lambda_q2, da_lambda_k2,
           da_subln, cmp_pe_k, cmp_w1_k, cmp_w2_k, cmp_pe_v, cmp_w1_v, cmp_w2_v, w_up, w_down):
    depth = w_in.shape[0]
    n_p, seq = x_prompt.shape[:2]
    nbatch, t = x_sample.shape[:2]
    n_pool = cache_da_k.shape[1]
    past = page_table.shape[1] * PAGE
    assert n_p == 1 and cache_da_k.shape[2] == PAGE
    xp = x_prompt.reshape(seq, D_MODEL)
    xs = x_sample.reshape(nbatch * t, D_MODEL)
    pos_p = jnp.arange(seq)
    pos_s = past + jnp.tile(jnp.arange(t), nbatch)
    w_keep = min(WINDOW, seq)
    p_rows, s_rows = [], []
    for layer in range(depth):
        lam_init = 0.8 - 0.6 * math.exp(-0.3 * layer)
        lams = [a[layer].reshape(1, DA_DIM).astype(F32)
                for a in (da_lambda_q1, da_lambda_k1, da_lambda_q2, da_lambda_k2)]
        subln = da_subln[layer].reshape(1, LANES).astype(F32)
        wp = _prep_w_in(w_in[layer])
        cwk = _prep_cmp(cmp_pe_k[layer], cmp_w1_k[layer], cmp_w2_k[layer])
        cwv = _prep_cmp(cmp_pe_v[layer], cmp_w1_v[layer], cmp_w2_v[layer])
        wa, wb = _prep_w_out(w_out[layer])
        wu, wd = w_up[layer].astype(BF16), w_down[layer].astype(BF16)
        post = (norm_mix_post[layer], norm_ffn_pre[layer], norm_ffn_post[layer], wa, wb, wu, wd)

        p = _project(xp, pos_p, wp, norm_mix_pre[layer], BF16, 512)
        pt = _project_t(xp, pos_p, _prep_w_in_t(w_in[layer]), norm_mix_pre[layer], 512)
        oa_p = _da_prompt(pt, p["ka_b"], lams, subln, lam_init, 256, 512, 2)
        ck = _compress_prompt(p["kc"], cwk)
        cv = _compress_prompt(p["vc"], cwv)
        ob_p = _nsa_prompt_t(pt, p["ks_b"], p["kw_b"], ck, cv, 256, 512)
        kv_t = lambda a: jnp.transpose(a.reshape(NSA_GROUPS, NSA_DIM, -1), (2, 0, 1))[None]
        p_rows.append((jnp.transpose(pt["kat"].reshape(DA_HEADS, 2, DA_DIM, seq), (3, 0, 1, 2))[None],
                       p["va"].reshape(1, seq, DA_HEADS, 2 * DA_DIM),
                       kv_t(pt["kct"]), kv_t(pt["vct"]), kv_t(pt["kst"]), kv_t(pt["vst"]),
                       kv_t(pt["kwt"][:, seq - w_keep:]), kv_t(pt["vwt"][:, seq - w_keep:])))

        s = _project(xs, pos_s, wp, norm_mix_pre[layer], F32, 512)
        da_k_t = jnp.transpose(cache_da_k[layer], (0, 2, 3, 4, 1)).reshape(n_pool, DA_OUT, PAGE)
        da_v_r = cache_da_v[layer].reshape(n_pool, PAGE * DA_HEADS, 2 * DA_DIM)
        oa_s = _da_sample(s, page_table, da_k_t, da_v_r, lams, subln, lam_init, t)
        pk_t = lambda c: jnp.transpose(c[layer], (0, 2, 3, 1)).reshape(-1, KV_W, c.shape[2])
        cks, cvs = _compress_sample(page_table, pk_t(cache_nsa_cmp_k), pk_t(cache_nsa_cmp_v), cwk, cwv)
        ob_s = _nsa_sample(s, cks, cvs, page_table, pk_t(cache_nsa_sel_k), pk_t(cache_nsa_sel_v),
                           pk_t(state_nsa_win_k), pk_t(state_nsa_win_v), past, t)
        kvs = lambda a: a.reshape(nbatch, t, NSA_GROUPS, NSA_DIM)
        nwk = jnp.concatenate([state_nsa_win_k[layer], kvs(s["kw"])], axis=1)[:, t:]
        nwv = jnp.concatenate([state_nsa_win_v[layer], kvs(s["vw"])], axis=1)[:, t:]
        s_rows.append((s["ka"].reshape(nbatch, t, DA_HEADS, 2, DA_DIM), s["va"].reshape(nbatch, t, DA_HEADS, 2 * DA_DIM),
                       kvs(s["kc"]), kvs(s["vc"]), kvs(s["ks"]), kvs(s["vs"]), nwk, nwv))

        xp = _finish(xp, oa_p, ob_p, *post, 256)
        xs = _finish(xs, oa_s, ob_s, *post, 256)

    p_out = [jnp.stack(list(c), axis=0) for c in zip(*p_rows)]
    s_out = [jnp.stack(list(c), axis=0) for c in zip(*s_rows)]
    return (xp.reshape(1, seq, D_MODEL), xs.reshape(nbatch, t, D_MODEL), *p_out, *s_out)
```

```python
import functools
import math

import jax
import jax.numpy as jnp
import numpy as np
from jax import lax
from jax.experimental import pallas as pl
from jax.experimental.pallas import tpu as pltpu

F32 = jnp.float32
BF16 = jnp.bfloat16

LANES = 128
D_MODEL = 1024
DA_HEADS = 4
DA_DIM = 64
DA_OUT = DA_HEADS * 2 * DA_DIM
NSA_HEADS = 8
NSA_GROUPS = 2
NSA_HPG = NSA_HEADS // NSA_GROUPS
NSA_DIM = 64
NSA_OUT = NSA_HEADS * NSA_DIM
KV_W = NSA_GROUPS * NSA_DIM
CMP_STRIDE = 16
CMP_BLOCK = 32
CMP_HIDDEN = 128
SEL_BLOCK = 64
CMP_PER_SEL = SEL_BLOCK // CMP_STRIDE
SEL_TOPK = 16
WINDOW = 512
N_BRANCH = 3
ROPE_THETA = 500000.0
ROPE_DIMS = 16
D_FF = 4 * D_MODEL
EPS = 1e-6
NEG = -1e30
PAGE = 128
SCALE = 0.125
LOG2E = math.log2(math.e)
HALF_W = CMP_STRIDE * KV_W

VMEM_LIMIT = 56 * 1024 * 1024

C_QA, C_KA, C_VA, C_QB = 0, 512, 1024, 1536
C_KV = 2560
C_GT = C_KV + 6 * KV_W
N_PROJ = C_GT + LANES


def _dot(a, b):
    return jnp.dot(a, b, preferred_element_type=F32)


def _dot_nt(a, b):
    return lax.dot_general(a, b, (((1,), (1,)), ((), ())), preferred_element_type=F32)


def _params(sem=None):
    return pltpu.CompilerParams(dimension_semantics=sem, vmem_limit_bytes=VMEM_LIMIT)


def _vmem_spec():
    return pl.BlockSpec(memory_space=pltpu.VMEM)


def _row_tile(rows, preferred):
    return preferred if rows % preferred == 0 else rows


def _prep_w_in(w):
    qa, ka, va = w[:, 0:512], w[:, 512:1024], w[:, 1024:1536]
    qb = w[:, 1536:2048].reshape(D_MODEL, NSA_HEADS, NSA_DIM)
    kv6 = w[:, 2048:2816]
    gl = w[:, 2816:2840]
    z = jnp.zeros((D_MODEL, NSA_DIM), w.dtype)
    qb_w = []
    for h in range(NSA_HEADS):
        qb_w += [qb[:, h], z] if h // NSA_HPG == 0 else [z, qb[:, h]]
    gl_p = jnp.concatenate([gl, jnp.zeros((D_MODEL, LANES - gl.shape[1]), w.dtype)], axis=1)
    return jnp.concatenate([qa, ka, va] + qb_w + [kv6, gl_p], axis=1).astype(BF16)


def _prep_cmp(pe, w1, w2):
    def expand(w1h):
        w = w1h.reshape(CMP_STRIDE, NSA_DIM, CMP_HIDDEN)
        z = jnp.zeros_like(w)
        g0 = jnp.concatenate([w, z], axis=-1)
        g1 = jnp.concatenate([z, w], axis=-1)
        return jnp.stack([g0, g1], axis=1).reshape(HALF_W, 2 * CMP_HIDDEN).astype(BF16)

    def tile_pe(p):
        return jnp.tile(p[:, None, :], (1, NSA_GROUPS, 1)).reshape(1, HALF_W).astype(F32)

    n = CMP_STRIDE * NSA_DIM
    z2 = jnp.zeros_like(w2)
    w2e = jnp.concatenate([jnp.concatenate([w2, z2], axis=1),
                           jnp.concatenate([z2, w2], axis=1)], axis=0).astype(BF16)
    return (tile_pe(pe[:CMP_STRIDE]), tile_pe(pe[CMP_STRIDE:]), expand(w1[:n]), expand(w1[n:]), w2e)


def _prep_w_out(w_out):
    wa = w_out[:DA_OUT]
    wb = w_out[DA_OUT:].reshape(NSA_HEADS, NSA_DIM, D_MODEL)
    order = [0, 4, 1, 5, 2, 6, 3, 7]
    wb = jnp.concatenate([wb[h] for h in order], axis=0)
    return wa.astype(BF16), wb.astype(BF16)


def _rope_tables(pos):
    half = ROPE_DIMS // 2
    inv = jnp.power(jnp.float32(ROPE_THETA), -jnp.arange(half, dtype=F32) * (2.0 / ROPE_DIMS))
    ang = pos.astype(F32)[:, None] * inv[None, :]
    cos, sin = jnp.cos(ang), jnp.sin(ang)
    n = pos.shape[0]
    one = jnp.ones((n, NSA_DIM - ROPE_DIMS), F32)
    zero = jnp.zeros((n, NSA_DIM - ROPE_DIMS), F32)
    z8 = jnp.zeros((n, half), F32)
    c = jnp.concatenate([cos, cos, one], axis=1)
    a = jnp.concatenate([-sin, z8, zero], axis=1)
    b = jnp.concatenate([z8, sin, zero], axis=1)
    return tuple(jnp.concatenate([t, t], axis=1) for t in (c, a, b))


def _expand_matrix_t(n_keys, n_blocks):
    k = np.arange(n_keys)[:, None] // SEL_BLOCK
    b = np.arange(n_blocks)[None, :]
    return jnp.asarray((k == b).astype(np.float32), dtype=BF16)


def _proj_kernel(x_ref, c_ref, a_ref, b_ref, g_ref, w_ref,
                 qa0_ref, qa1_ref, qb_ref, ka_ref, va_ref,
                 kc_ref, vc_ref, ks_ref, vs_ref, kw_ref, vw_ref, gt_ref,
                 kab_ref, vab_ref, ksb_ref, vsb_ref, kwb_ref, vwb_ref):
    x = x_ref[...]
    ms = jnp.mean(x * x, axis=-1, keepdims=True)
    h = (x * lax.rsqrt(ms + EPS) * g_ref[...]).astype(BF16)
    c, a, b = c_ref[...], a_ref[...], b_ref[...]

    def rope(z):
        return z * c + pltpu.roll(z, LANES - ROPE_DIMS // 2, 1) * a + pltpu.roll(z, ROPE_DIMS // 2, 1) * b

    def sect(col, width):
        return _dot(h, w_ref[:, col:col + width])

    lane = lax.broadcasted_iota(jnp.int32, (x.shape[0], LANES), 1)
    lo = lane < DA_DIM

    z = sect(C_QA, DA_OUT)
    for i in range(DA_HEADS):
        q = rope(z[:, i * LANES:(i + 1) * LANES]) * SCALE
        qa0_ref[:, i * LANES:(i + 1) * LANES] = jnp.where(lo, q, 0.0).astype(qa0_ref.dtype)
        qa1_ref[:, i * LANES:(i + 1) * LANES] = jnp.where(lo, 0.0, q).astype(qa1_ref.dtype)
    z = sect(C_KA, DA_OUT)
    for i in range(DA_HEADS):
        k = rope(z[:, i * LANES:(i + 1) * LANES])
        ka_ref[:, i * LANES:(i + 1) * LANES] = k
        kab_ref[:, i * LANES:(i + 1) * LANES] = k.astype(BF16)
    z = sect(C_VA, DA_OUT)
    va_ref[...] = z
    vab_ref[...] = z.astype(BF16)
    z = sect(C_QB, NSA_HEADS * LANES)
    for i in range(NSA_HEADS):
        qb_ref[:, i * LANES:(i + 1) * LANES] = (rope(z[:, i * LANES:(i + 1) * LANES]) * SCALE).astype(qb_ref.dtype)
    z = sect(C_KV, 6 * KV_W)
    kc_ref[...] = rope(z[:, 0:KV_W])
    vc_ref[...] = z[:, KV_W:2 * KV_W]
    k = rope(z[:, 2 * KV_W:3 * KV_W])
    ks_ref[...] = k
    ksb_ref[...] = k.astype(BF16)
    v = z[:, 3 * KV_W:4 * KV_W]
    vs_ref[...] = v
    vsb_ref[...] = v.astype(BF16)
    k = rope(z[:, 4 * KV_W:5 * KV_W])
    kw_ref[...] = k
    kwb_ref[...] = k.astype(BF16)
    v = z[:, 5 * KV_W:6 * KV_W]
    vw_ref[...] = v
    vwb_ref[...] = v.astype(BF16)
    gl = sect(C_GT, LANES)
    gt_ref[...] = 1.0 / (1.0 + jnp.exp(-gl))


def _project(x2d, pos, wp, g_pre, q_dtype, tm):
    rows = x2d.shape[0]
    tm = _row_tile(rows, tm)
    tabs = _rope_tables(pos)
    row_blk = lambda w: pl.BlockSpec((tm, w), lambda i: (i, 0))
    full = lambda shp: pl.BlockSpec(shp, lambda i: (0, 0))
    widths_f32 = [DA_OUT, DA_OUT] + [KV_W] * 6 + [LANES]
    widths_b16 = [DA_OUT, DA_OUT] + [KV_W] * 4
    out_shape = ([jax.ShapeDtypeStruct((rows, DA_OUT), q_dtype)] * 2
                 + [jax.ShapeDtypeStruct((rows, NSA_HEADS * LANES), q_dtype)]
                 + [jax.ShapeDtypeStruct((rows, w), F32) for w in widths_f32]
                 + [jax.ShapeDtypeStruct((rows, w), BF16) for w in widths_b16])
    out_specs = ([row_blk(DA_OUT)] * 2 + [row_blk(NSA_HEADS * LANES)]
                 + [row_blk(w) for w in widths_f32] + [row_blk(w) for w in widths_b16])
    outs = pl.pallas_call(
        _proj_kernel,
        grid=(rows // tm,),
        in_specs=[row_blk(D_MODEL), row_blk(LANES), row_blk(LANES), row_blk(LANES),
                  full((1, D_MODEL)), full((D_MODEL, N_PROJ))],
        out_specs=out_specs,
        out_shape=out_shape,
        compiler_params=_params(("arbitrary",)),
        name="proj",
    )(x2d, *tabs, g_pre.reshape(1, D_MODEL).astype(F32), wp)
    names = ("qa0", "qa1", "qb", "ka", "va", "kc", "vc", "ks", "vs", "kw", "vw", "gt",
             "ka_b", "va_b", "ks_b", "vs_b", "kw_b", "vw_b")
    return dict(zip(names, outs))


R_QA, R_QB, R_KA, R_VA, R_KV = 0, 512, 1536, 2048, 2560
R_GT = R_KV + 6 * KV_W
N_GATES = NSA_HEADS * N_BRANCH
N_PROJ_T = R_GT + N_GATES


def _prep_w_in_t(w):
    wp = _prep_w_in(w)
    return jnp.concatenate([wp[:, C_QA:C_KA], wp[:, C_QB:C_KV], wp[:, C_KA:C_QB],
                            wp[:, C_KV:C_GT + N_GATES]], axis=1).T


def _rope_tables_t(pos):
    return tuple(tab.T for tab in _rope_tables(pos))


def _proj_t_kernel(x_ref, c_ref, a_ref, b_ref, g_ref, wt_ref,
                   qa0t_ref, qa1t_ref, qbt_ref, vat_b_ref, vst_b_ref, vwt_b_ref,
                   kat_ref, kct_ref, vct_ref, kst_ref, vst_ref, kwt_ref, vwt_ref, gtt_ref):
    x = x_ref[...]
    ms = jnp.mean(x * x, axis=-1, keepdims=True)
    h = (x * lax.rsqrt(ms + EPS) * g_ref[...]).astype(BF16)
    c, a, b = c_ref[...], a_ref[...], b_ref[...]

    def rope(z):
        return z * c + pltpu.roll(z, LANES - ROPE_DIMS // 2, 0) * a + pltpu.roll(z, ROPE_DIMS // 2, 0) * b

    def sect(r, n):
        return _dot_nt(wt_ref[r:r + n, :], h)

    def grp(z, i):
        return z[i * LANES:(i + 1) * LANES]

    lo = lax.broadcasted_iota(jnp.int32, (LANES, x.shape[0]), 0) < DA_DIM
    z = sect(R_QA, DA_OUT)
    for i in range(DA_HEADS):
        q = rope(grp(z, i)) * (SCALE * LOG2E)
        qa0t_ref[i * LANES:(i + 1) * LANES, :] = jnp.where(lo, q, 0.0).astype(BF16)
        qa1t_ref[i * LANES:(i + 1) * LANES, :] = jnp.where(lo, 0.0, q).astype(BF16)
    z = sect(R_QB, NSA_HEADS * LANES)
    for i in range(NSA_HEADS):
        qbt_ref[i * LANES:(i + 1) * LANES, :] = (rope(grp(z, i)) * (SCALE * LOG2E)).astype(BF16)
    z = sect(R_KA, DA_OUT)
    for i in range(DA_HEADS):
        kat_ref[i * LANES:(i + 1) * LANES, :] = rope(grp(z, i))
    vat_b_ref[...] = sect(R_VA, DA_OUT).astype(BF16)
    z = sect(R_KV, 6 * KV_W)
    kct_ref[...] = rope(grp(z, 0))
    vct_ref[...] = grp(z, 1)
    kst_ref[...] = rope(grp(z, 2))
    vst_ref[...] = grp(z, 3)
    vst_b_ref[...] = grp(z, 3).astype(BF16)
    kwt_ref[...] = rope(grp(z, 4))
    vwt_ref[...] = grp(z, 5)
    vwt_b_ref[...] = grp(z, 5).astype(BF16)
    gtt_ref[...] = 1.0 / (1.0 + jnp.exp(-sect(R_GT, N_GATES)))


def _project_t(x2d, pos, wpt, g_pre, tm):
    rows = x2d.shape[0]
    tm = _row_tile(rows, tm)
    tabs = _rope_tables_t(pos)
    col_blk = lambda n: pl.BlockSpec((n, tm), lambda i: (0, i))
    full = lambda shp: pl.BlockSpec(shp, lambda i: (0, 0))
    heights = ([(DA_OUT, BF16)] * 2 + [(NSA_HEADS * LANES, BF16), (DA_OUT, BF16), (KV_W, BF16), (KV_W, BF16),
                                      (DA_OUT, F32)] + [(KV_W, F32)] * 6 + [(N_GATES, F32)])
    outs = pl.pallas_call(
        _proj_t_kernel,
        grid=(rows // tm,),
        in_specs=[pl.BlockSpec((tm, D_MODEL), lambda i: (i, 0)), col_blk(LANES), col_blk(LANES), col_blk(LANES),
                  full((1, D_MODEL)), full((N_PROJ_T, D_MODEL))],
        out_specs=[col_blk(n) for n, _ in heights],
        out_shape=[jax.ShapeDtypeStruct((n, rows), dt) for n, dt in heights],
        compiler_params=_params(("arbitrary",)),
        name="proj_t",
    )(x2d, *tabs, g_pre.reshape(1, D_MODEL).astype(F32), wpt)
    names = ("qa0t", "qa1t", "qbt", "vat_b", "vst_b", "vwt_b", "kat", "kct", "vct", "kst", "vst", "kwt", "vwt", "gtt")
    return dict(zip(names, outs))


def _compress_halves(rows, pet, peb, w1t, w1b, w2):
    nh = rows.shape[0]
    top = _dot((rows + pet).astype(BF16), w1t)
    bot = _dot((rows + peb).astype(BF16), w1b)
    hid = top + pltpu.roll(bot, nh - 1, 0)
    act = hid * (1.0 / (1.0 + jnp.exp(-hid)))
    return _dot(act.astype(BF16), w2)


def _store_permuted(tok, tmp_ref, out_ref):
    nb = tok.shape[0] // CMP_PER_SEL
    tmp_ref[...] = tok
    for r in range(CMP_PER_SEL):
        out_ref[r * nb:(r + 1) * nb, :] = tmp_ref[pl.ds(r, nb, stride=CMP_PER_SEL), :].astype(out_ref.dtype)


def _cmp_prompt_kernel(rows_ref, pet_ref, peb_ref, w1t_ref, w1b_ref, w2_ref, out_ref, tmp_ref):
    tok = _compress_halves(rows_ref[...], pet_ref[...], peb_ref[...], w1t_ref[...], w1b_ref[...], w2_ref[...])
    _store_permuted(tok, tmp_ref, out_ref)


def _compress_prompt(rows, cw):
    nh = rows.shape[0] // CMP_STRIDE
    return pl.pallas_call(
        _cmp_prompt_kernel,
        in_specs=[_vmem_spec()] * 6,
        out_specs=_vmem_spec(),
        out_shape=jax.ShapeDtypeStruct((nh, KV_W), BF16),
        scratch_shapes=[pltpu.VMEM((nh, KV_W), F32)],
        compiler_params=_params(),
        name="cmp_prompt",
    )(rows.reshape(nh, HALF_W), *cw)


def _lambda(lq1, lk1, lq2, lk2, lam_init):
    return (jnp.exp(jnp.sum(lq1 * lk1, axis=-1, keepdims=True))
            - jnp.exp(jnp.sum(lq2 * lk2, axis=-1, keepdims=True)) + lam_init)


def _subln(o, g, lam_init):
    ms = jnp.mean(o * o, axis=-1, keepdims=True)
    return o * lax.rsqrt(ms + EPS) * g * (1.0 - lam_init)


ONES_ROWS = 16


def _da_prompt_kernel(lq1_ref, lk1_ref, lq2_ref, lk2_ref, sg_ref, q0t_ref, q1t_ref, k_ref, vt_ref, o_ref,
                      *scratch, tq, tk, nh, lam_init):
    m_scs, acc_scs = scratch[:nh], scratch[nh:]
    qi = pl.program_id(1)
    q0 = qi * tq
    lam = _lambda(lq1_ref[...], lk1_ref[...], lq2_ref[...], lk2_ref[...], lam_init)
    qcats = [jnp.concatenate([q0t_ref[hd * LANES:(hd + 1) * LANES, :], q1t_ref[hd * LANES:(hd + 1) * LANES, :]],
                             axis=1) for hd in range(nh)]
    for hd in range(nh):
        m_scs[hd][...] = jnp.full(m_scs[hd].shape, NEG, F32)
        acc_scs[hd][...] = jnp.zeros(acc_scs[hd].shape, F32)

    def tile(start, n, masked):
        items = [(sb, hd) for sb in range(n // tq) for hd in range(nh)]

        def at(i):
            sb, hd = items[i]
            return pl.multiple_of(start + sb * tq, tq), hd

        def score(i):
            s0, hd = at(i)
            return _dot(k_ref[pl.ds(s0, tq), hd * LANES:(hd + 1) * LANES], qcats[hd])

        def probs(i, st):
            s0, hd = at(i)
            if masked:
                kpos = s0 + lax.broadcasted_iota(jnp.int32, st.shape, 0)
                qpos = q0 + (lax.broadcasted_iota(jnp.int32, st.shape, 1) & (tq - 1))
                st = jnp.where(kpos <= qpos, st, NEG)
            m_old = m_scs[hd][...]
            m_new = jnp.maximum(m_old, jnp.max(st, axis=0, keepdims=True))
            m_scs[hd][...] = m_new
            return jnp.exp2(st - m_new[0:1]).astype(BF16), jnp.exp2(m_old - m_new)

        def pv(i, pt):
            s0, hd = at(i)
            vt = jnp.concatenate([vt_ref[hd * LANES:(hd + 1) * LANES, pl.ds(s0, tq)],
                                  jnp.ones((ONES_ROWS, tq), BF16)], axis=0)
            return _dot(vt, pt)

        def finish(i, alpha, o):
            hd = items[i][1]
            acc_scs[hd][...] = acc_scs[hd][...] * alpha[0:1] + o

        _staggered(len(items), score, probs, pv, finish)

    def body(kj, carry):
        tile(pl.multiple_of(kj * tk, tk), tk, False)
        return carry

    n_big = q0 // tk
    lax.fori_loop(0, n_big, body, 0)
    for d in range(tk // tq - 1):
        @pl.when(q0 - n_big * tk > d * tq)
        def _():
            tile(pl.multiple_of(n_big * tk + d * tq, tq), tq, False)
    tile(pl.multiple_of(q0, tq), tq, True)
    for hd in range(nh):
        a0, a1 = acc_scs[hd][:, :tq], acc_scs[hd][:, tq:]
        ot = a0[:LANES] / a0[LANES:LANES + 1] - lam * (a1[:LANES] / a1[LANES:LANES + 1])
        ms = jnp.mean(ot * ot, axis=0, keepdims=True)
        yt = ot * lax.rsqrt(ms + EPS) * sg_ref[...] * (1.0 - lam_init)
        o_ref[:, hd * LANES:(hd + 1) * LANES] = yt.T


def _da_prompt(pt, ka_b, lams, subln, lam_init, tq, tk, nh):
    seq = ka_b.shape[0]
    assert seq % tq == 0 and tk % tq == 0 and tq & (tq - 1) == 0 and DA_HEADS % nh == 0
    w = nh * LANES
    vec = lambda: pl.BlockSpec((1, DA_DIM), lambda h, i: (0, 0))
    qblk = pl.BlockSpec((w, tq), lambda h, i: (h, i))
    sg = jnp.broadcast_to(subln.reshape(LANES, 1), (LANES, tq))
    return pl.pallas_call(
        functools.partial(_da_prompt_kernel, tq=tq, tk=tk, nh=nh, lam_init=lam_init),
        grid=(DA_HEADS // nh, seq // tq),
        in_specs=[vec(), vec(), vec(), vec(), pl.BlockSpec((LANES, tq), lambda h, i: (0, 0)),
                  qblk, qblk, pl.BlockSpec((seq, w), lambda h, i: (0, h)),
                  pl.BlockSpec((w, seq), lambda h, i: (h, 0))],
        out_specs=pl.BlockSpec((tq, w), lambda h, i: (i, h)),
        out_shape=jax.ShapeDtypeStruct((seq, DA_OUT), F32),
        scratch_shapes=([pltpu.VMEM((8, 2 * tq), F32)] * nh
                        + [pltpu.VMEM((LANES + ONES_ROWS, 2 * tq), F32)] * nh),
        compiler_params=_params(("arbitrary", "arbitrary")),
        name="da_prompt",
    )(*lams, sg, pt["qa0t"], pt["qa1t"], ka_b, pt["vat_b"])


def _softmax_rows(s, mask):
    s = jnp.where(mask, s, NEG)
    m = jnp.max(s, axis=-1, keepdims=True)
    e = jnp.where(mask, jnp.exp(s - m), 0.0)
    l = jnp.sum(e, axis=-1, keepdims=True)
    return e * (1.0 / jnp.maximum(l, 1e-30))


def _select_blocks(imp, qblk):
    nb = imp.shape[1]
    jb = lax.broadcasted_iota(jnp.int32, imp.shape, 1)
    jbf = jb.astype(F32)
    forced = (jb == 0) | (jb == qblk) | (jb == qblk - 1)
    valid = jb <= qblk
    score = jnp.where(forced, 1e30, jnp.where(valid, imp, -1.0))
    picked = jnp.zeros(imp.shape, F32)
    for _ in range(min(SEL_TOPK, nb)):
        mx = jnp.max(score, axis=-1, keepdims=True)
        first = jnp.min(jnp.where(score == mx, jbf, 1e9), axis=-1, keepdims=True)
        hit = jbf == first
        picked = jnp.where(hit, 1.0, picked)
        score = jnp.where(hit, -2.0, score)
    return jnp.where(valid, picked, 0.0)


def _softmax_cols(st, mask):
    st = jnp.where(mask, st, NEG)
    m = jnp.max(st, axis=0, keepdims=True)
    e = jnp.where(mask, jnp.exp2(st - m), 0.0)
    l = jnp.sum(e, axis=0, keepdims=True)
    return e * (1.0 / jnp.maximum(l, 1e-30))


def _select_blocks_t(imp_t, qblk):
    nb = imp_t.shape[0]
    jb = lax.broadcasted_iota(jnp.int32, imp_t.shape, 0)
    jbf = jb.astype(F32)
    forced = (jb == 0) | (jb == qblk) | (jb == qblk - 1)
    valid = jb <= qblk
    score = jnp.where(forced, 1e30, jnp.where(valid, imp_t, -1.0))
    picked = jnp.zeros(imp_t.shape, F32)
    for _ in range(min(SEL_TOPK, nb)):
        mx = jnp.max(score, axis=0, keepdims=True)
        first = jnp.min(jnp.where(score == mx, jbf, 1e9), axis=0, keepdims=True)
        hit = jbf == first
        picked = jnp.where(hit, 1.0, picked)
        score = jnp.where(hit, -2.0, score)
    return jnp.where(valid, picked, 0.0)


def _staggered(n, score, probs, pv, finish):
    s_next = score(0)
    pending = None
    for i in range(n):
        s_cur = s_next
        if i + 1 < n:
            s_next = score(i + 1)
        p, aux = probs(i, s_cur)
        out = pv(i, p)
        if pending is not None:
            finish(*pending)
        pending = (i, aux, out)
    finish(*pending)


def _nsa_prompt_t_kernel(cend_ref, qt_ref, gt_ref, ck_ref, cvt_ref, ks_ref, vst_ref, kw_ref, vwt_ref, o_ref,
                         sel_sc, res_sc, *scratch, tq, tk):
    m_scs, acc_scs = scratch[:NSA_HEADS], scratch[NSA_HEADS:]
    qi = pl.program_id(0)
    q0 = qi * tq
    nb = sel_sc.shape[1]
    qpos = q0 + lax.broadcasted_iota(jnp.int32, (1, tq), 1)

    def q_head(h):
        return qt_ref[h * LANES:(h + 1) * LANES, :]

    def gate(h, br):
        i = h * N_BRANCH + br
        return gt_ref[i:i + 1, :]

    def with_ones(vt):
        return jnp.concatenate([vt, jnp.ones((ONES_ROWS, vt.shape[1]), BF16)], axis=0)

    ck = ck_ref[...]
    cvt = cvt_ref[...]
    cmask = cend_ref[...] <= qpos
    psums = [None] * NSA_GROUPS

    def cmp_probs(h, st):
        p = _softmax_cols(st, cmask)
        g = h // NSA_HPG
        psums[g] = p if psums[g] is None else psums[g] + p
        return p.astype(BF16), None

    def cmp_finish(h, _, o):
        res_sc[h] = o * gate(h, 0)

    _staggered(NSA_HEADS, lambda h: _dot(ck, q_head(h)), cmp_probs, lambda h, p: _dot(cvt, p), cmp_finish)
    imps = []
    for g in range(NSA_GROUPS):
        imp = psums[g][0:nb]
        for r in range(1, CMP_PER_SEL):
            imp = imp + psums[g][r * nb:(r + 1) * nb]
        imps.append(imp)
    sel = _select_blocks_t(jnp.concatenate(imps, axis=1), jnp.concatenate([qpos, qpos], axis=1) // SEL_BLOCK)
    for g in range(NSA_GROUPS):
        sel_sc[g] = sel[:, g * tq:(g + 1) * tq]

    for h in range(NSA_HEADS):
        m_scs[h][...] = jnp.full(m_scs[h].shape, NEG, F32)
        acc_scs[h][...] = jnp.zeros(acc_scs[h].shape, F32)

    def tile(start, n, causal, sub):
        items = [(sb, h) for sb in range(n // sub) for h in range(NSA_HEADS)]
        masks = {}

        def s0(sb):
            return pl.multiple_of(start + sb * sub, sub)

        def allowed(sb, g):
            if (sb, g) not in masks:
                b0 = s0(sb) // SEL_BLOCK
                rows = [jnp.broadcast_to(sel_sc[g, pl.ds(b0 + i, 1), :], (SEL_BLOCK, tq))
                        for i in range(sub // SEL_BLOCK)]
                ok = jnp.concatenate(rows, axis=0) > 0.5
                if causal:
                    kpos = s0(sb) + lax.broadcasted_iota(jnp.int32, (sub, tq), 0)
                    ok = ok & (kpos <= qpos)
                masks[(sb, g)] = ok
            return masks[(sb, g)]

        def score(i):
            sb, h = items[i]
            return _dot(ks_ref[pl.ds(s0(sb), sub), :], q_head(h))

        def sel_probs(i, st):
            sb, h = items[i]
            st = jnp.where(allowed(sb, h // NSA_HPG), st, NEG)
            m_old = m_scs[h][...]
            m_new = jnp.maximum(m_old, jnp.max(st, axis=0, keepdims=True))
            m_scs[h][...] = m_new
            return jnp.exp2(st - m_new[0:1]).astype(BF16), jnp.exp2(m_old - m_new)

        def sel_pv(i, p):
            return _dot(with_ones(vst_ref[:, pl.ds(s0(items[i][0]), sub)]), p)

        def sel_finish(i, alpha, o):
            h = items[i][1]
            acc_scs[h][...] = acc_scs[h][...] * alpha[0:1] + o

        _staggered(len(items), score, sel_probs, sel_pv, sel_finish)

    def body(kj, carry):
        tile(pl.multiple_of(kj * tk, tk), tk, False, min(tk, 2 * tq))
        return carry

    n_big = q0 // tk
    lax.fori_loop(0, n_big, body, 0)
    for d in range(tk // tq - 1):
        @pl.when(q0 - n_big * tk > d * tq)
        def _():
            tile(pl.multiple_of(n_big * tk + d * tq, tq), tq, False, tq)
    tile(pl.multiple_of(q0, tq), tq, True, tq)

    wlen = WINDOW + tq
    wstart = pl.multiple_of(jnp.maximum(q0 - WINDOW, 0), tq)
    kw = kw_ref[pl.ds(wstart, wlen), :]
    vwt = with_ones(vwt_ref[:, pl.ds(wstart, wlen)])
    kpos = wstart + lax.broadcasted_iota(jnp.int32, (wlen, tq), 0)
    wmask = (kpos <= qpos) & (kpos > qpos - WINDOW)
    def win_probs(h, sw):
        sw = jnp.where(wmask, sw, NEG)
        return jnp.exp2(sw - jnp.max(sw, axis=0, keepdims=True)).astype(BF16), None

    def win_finish(h, _, ow):
        acc = acc_scs[h][...]
        res_sc[h] = (res_sc[h] + (acc[:LANES] / acc[LANES:LANES + 1]) * gate(h, 1)
                     + (ow[:LANES] / ow[LANES:LANES + 1]) * gate(h, 2))

    _staggered(NSA_HEADS, lambda h: _dot(kw, q_head(h)), win_probs, lambda h, p: _dot(vwt, p), win_finish)
    lo = lax.broadcasted_iota(jnp.int32, (LANES, tq), 0) < NSA_DIM
    for h in range(NSA_HPG):
        o_ref[:, h * LANES:(h + 1) * LANES] = jnp.where(lo, res_sc[h], res_sc[h + NSA_HPG]).T


def _nsa_prompt_t(pt, ks_b, kw_b, ck, cv, tq, tk):
    seq = ks_b.shape[0]
    nc = seq // CMP_STRIDE
    nb = seq // SEL_BLOCK
    assert seq % tq == 0 and tk % tq == 0 and WINDOW % tq == 0 and seq >= WINDOW + tq and tq % SEL_BLOCK == 0
    pp = np.arange(nc)
    cend = ((CMP_PER_SEL * (pp % nb) + pp // nb) * CMP_STRIDE + (CMP_BLOCK - 1)).astype(np.int32)
    cend = jnp.asarray(np.broadcast_to(cend[:, None], (nc, tq)))
    return pl.pallas_call(
        functools.partial(_nsa_prompt_t_kernel, tq=tq, tk=tk),
        grid=(seq // tq,),
        in_specs=[_vmem_spec(),
                  pl.BlockSpec((NSA_HEADS * LANES, tq), lambda i: (0, i)),
                  pl.BlockSpec((N_GATES, tq), lambda i: (0, i))] + [_vmem_spec()] * 6,
        out_specs=pl.BlockSpec((tq, NSA_OUT), lambda i: (i, 0)),
        out_shape=jax.ShapeDtypeStruct((seq, NSA_OUT), F32),
        scratch_shapes=([pltpu.VMEM((NSA_GROUPS, nb, tq), F32), pltpu.VMEM((NSA_HEADS, LANES, tq), F32)]
                        + [pltpu.VMEM((8, tq), F32)] * NSA_HEADS
                        + [pltpu.VMEM((LANES + ONES_ROWS, tq), F32)] * NSA_HEADS),
        compiler_params=_params(("arbitrary",)),
        name="nsa_prompt",
    )(cend, pt["qbt"], pt["gtt"], ck, cv.T, ks_b, pt["vst_b"], kw_b, pt["vwt_b"])


def _page_copies(pt_ref, seq, first_page, n_pages, slot, streams, sems):
    out = []
    for j in range(n_pages):
        pg = pt_ref[seq, first_page + j]
        for i, (cache, dst) in enumerate(streams):
            out.append(pltpu.make_async_copy(cache.at[pg], dst(slot, j), sems.at[i, slot]))
    return out


def _rows_dst(buf, rows):
    return lambda slot, j: buf.at[slot, pl.ds(j * rows, rows)]


def _cols_dst(buf, cols):
    return lambda slot, j: buf.at[slot, :, pl.ds(j * cols, cols)]


def _online_update(state, s, pv):
    m_old, l_old, acc = state
    m_new = jnp.maximum(m_old, jnp.max(s, axis=-1, keepdims=True))
    alpha = jnp.exp(m_old - m_new)
    p = jnp.exp(s - m_new)
    l_new = alpha * l_old + jnp.sum(p, axis=-1, keepdims=True)
    return m_new, l_new, alpha * acc + pv(p.astype(BF16))


def _pad_rows(x, rows):
    return jnp.concatenate([x, jnp.zeros((rows - x.shape[0], x.shape[1]), x.dtype)], axis=0)


def _da_sample_kernel(pt_ref, lq1_ref, lk1_ref, lq2_ref, lk2_ref, sg_ref, q0_ref, q1_ref, kn_ref, vn_ref,
                      kc_hbm, vc_hbm, o_ref, kbuf, vbuf, sems, *, ch, nch, lam_init):
    b = pl.program_id(0)
    nbatch = pl.num_programs(0)
    t = q0_ref.shape[0]
    lam = _lambda(lq1_ref[...], lk1_ref[...], lq2_ref[...], lk2_ref[...], lam_init)

    streams = ((kc_hbm, _cols_dst(kbuf, PAGE)), (vc_hbm, _rows_dst(vbuf, PAGE * DA_HEADS)))

    def copies(seq, c, slot):
        return _page_copies(pt_ref, seq, c * ch, ch, slot, streams, sems)

    @pl.when(b == 0)
    def _():
        for cp in copies(0, 0, 0):
            cp.start()

    lane = lax.broadcasted_iota(jnp.int32, (t, DA_OUT), 1)
    parts = []
    for h in range(DA_HEADS):
        in_head = (lane >= h * LANES) & (lane < (h + 1) * LANES)
        parts += [jnp.where(in_head, q0_ref[...], 0.0), jnp.where(in_head, q1_ref[...], 0.0)]
    qbd = jnp.concatenate(parts, axis=0).astype(BF16)
    nrow = qbd.shape[0]
    hrows = 2 * t

    state = (jnp.full((nrow, 1), NEG, F32), jnp.zeros((nrow, 1), F32), jnp.zeros((nrow, LANES), F32))
    for c in range(nch):
        slot = c % 2
        if c + 1 < nch:
            for cp in copies(b, c + 1, (c + 1) % 2):
                cp.start()
        else:
            @pl.when(b + 1 < nbatch)
            def _():
                for cp in copies(b + 1, 0, 0):
                    cp.start()
        for cp in copies(b, c, slot):
            cp.wait()

        def pv(p, slot=slot):
            return jnp.concatenate(
                [_dot(p[h * hrows:(h + 1) * hrows],
                      vbuf[slot, pl.ds(h, ch * PAGE, stride=DA_HEADS), :].astype(BF16))
                 for h in range(DA_HEADS)], axis=0)

        state = _online_update(state, _dot(qbd, kbuf[slot].astype(BF16)), pv)

    kn = _pad_rows(kn_ref[...], LANES).astype(BF16)
    vn = _pad_rows(vn_ref[...], LANES).astype(BF16)
    col = lax.broadcasted_iota(jnp.int32, (nrow, LANES), 1)
    rowq = lax.broadcasted_iota(jnp.int32, (nrow, LANES), 0) & (t - 1)
    s_n = jnp.where((col < t) & (col <= rowq), _dot_nt(qbd, kn), NEG)

    def pv_new(p):
        full = _dot(p, vn)
        return jnp.concatenate([full[h * hrows:(h + 1) * hrows, h * LANES:(h + 1) * LANES]
                                for h in range(DA_HEADS)], axis=0)

    _, l, acc = _online_update(state, s_n, pv_new)
    out = acc / l
    for h in range(DA_HEADS):
        r = h * hrows
        o = out[r:r + t] - lam * out[r + t:r + 2 * t]
        o_ref[:, h * LANES:(h + 1) * LANES] = _subln(o, sg_ref[...], lam_init)


def _da_sample(s, page_table, cache_k, cache_v, lams, subln, lam_init, t):
    nbatch, n_pages = page_table.shape
    assert t == 8 and n_pages % 2 == 0
    ch = min(16, n_pages // 2)
    nch = n_pages // ch
    assert nch % 2 == 0 and nch * ch == n_pages
    vec = lambda w: pl.BlockSpec((1, w), lambda b, pt: (0, 0))
    blk = pl.BlockSpec((t, DA_OUT), lambda b, pt: (b, 0))
    anyspec = pl.BlockSpec(memory_space=pl.ANY)
    grid_spec = pltpu.PrefetchScalarGridSpec(
        num_scalar_prefetch=1,
        grid=(nbatch,),
        in_specs=[vec(DA_DIM)] * 4 + [vec(LANES), blk, blk, blk, blk, anyspec, anyspec],
        out_specs=blk,
        scratch_shapes=[pltpu.VMEM((2, DA_OUT, ch * PAGE), F32), pltpu.VMEM((2, ch * PAGE * DA_HEADS, LANES), F32),
                        pltpu.SemaphoreType.DMA((2, 2))],
    )
    return pl.pallas_call(
        functools.partial(_da_sample_kernel, ch=ch, nch=nch, lam_init=lam_init),
        grid_spec=grid_spec,
        out_shape=jax.ShapeDtypeStruct((nbatch * t, DA_OUT), F32),
        compiler_params=_params(("arbitrary",)),
        name="da_sample",
    )(page_table, *lams, subln, s["qa0"], s["qa1"], s["ka"], s["va"], cache_k, cache_v)


def _cmp_sample_kernel(pt_ref, petk_ref, pebk_ref, w1tk_ref, w1bk_ref, w2k_ref,
                       petv_ref, pebv_ref, w1tv_ref, w1bv_ref, w2v_ref, kc_hbm, vc_hbm,
                       ck_ref, cv_ref, kbuf, vbuf, krows_ref, vrows_ref, tmp_ref, sems, *, n_pages):
    b = pl.program_id(0)
    nbatch = pl.num_programs(0)
    slot = b % 2
    page_dst = lambda buf: (lambda sl, j: buf.at[sl, j])
    streams = ((kc_hbm, page_dst(kbuf)), (vc_hbm, page_dst(vbuf)))

    def copies(seq, sl):
        return _page_copies(pt_ref, seq, 0, n_pages, sl, streams, sems)

    @pl.when(b == 0)
    def _():
        for cp in copies(0, 0):
            cp.start()

    @pl.when(b + 1 < nbatch)
    def _():
        for cp in copies(b + 1, 1 - slot):
            cp.start()

    for cp in copies(b, slot):
        cp.wait()

    def to_rows(buf, rows_ref, pages):
        for j in pages:
            rows_ref[j * PAGE:(j + 1) * PAGE, :] = buf[slot, j].T

    def compress(rows_ref, pet_ref, peb_ref, w1t_ref, w1b_ref, w2_ref, out_ref, between):
        nh = rows_ref.shape[0] // CMP_STRIDE
        top = jnp.zeros((nh, 2 * CMP_HIDDEN), F32)
        bot = jnp.zeros((nh, 2 * CMP_HIDDEN), F32)
        for j in range(0, CMP_STRIDE, 2):
            x = jnp.concatenate([rows_ref[pl.ds(j, nh, stride=CMP_STRIDE), :],
                                 rows_ref[pl.ds(j + 1, nh, stride=CMP_STRIDE), :]], axis=1)
            c0, c1 = j * KV_W, (j + 2) * KV_W
            top = top + _dot((x + pet_ref[:, c0:c1]).astype(BF16), w1t_ref[c0:c1, :])
            bot = bot + _dot((x + peb_ref[:, c0:c1]).astype(BF16), w1b_ref[c0:c1, :])
            between(j // 2)
        hid = top + pltpu.roll(bot, nh - 1, 0)
        act = hid * (1.0 / (1.0 + jnp.exp(-hid)))
        _store_permuted(_dot(act.astype(BF16), w2_ref[...]), tmp_ref, out_ref)

    steps = CMP_STRIDE // 2
    per_step = -(-n_pages // steps)
    to_rows(kbuf, krows_ref, range(n_pages))
    compress(krows_ref, petk_ref, pebk_ref, w1tk_ref, w1bk_ref, w2k_ref, ck_ref,
             lambda i: to_rows(vbuf, vrows_ref, range(min(i * per_step, n_pages), min((i + 1) * per_step, n_pages))))
    compress(vrows_ref, petv_ref, pebv_ref, w1tv_ref, w1bv_ref, w2v_ref, cv_ref, lambda i: None)


def _compress_sample(page_table, cache_k, cache_v, cwk, cwv):
    nbatch, n_pages = page_table.shape
    nh = n_pages * PAGE // CMP_STRIDE
    out_blk = pl.BlockSpec((None, nh, KV_W), lambda b, pt: (b, 0, 0))
    anyspec = pl.BlockSpec(memory_space=pl.ANY)
    grid_spec = pltpu.PrefetchScalarGridSpec(
        num_scalar_prefetch=1,
        grid=(nbatch,),
        in_specs=[_vmem_spec()] * 10 + [anyspec, anyspec],
        out_specs=[out_blk, out_blk],
        scratch_shapes=[pltpu.VMEM((2, n_pages, KV_W, PAGE), F32), pltpu.VMEM((2, n_pages, KV_W, PAGE), F32),
                        pltpu.VMEM((n_pages * PAGE, KV_W), F32), pltpu.VMEM((n_pages * PAGE, KV_W), F32),
                        pltpu.VMEM((nh, KV_W), F32), pltpu.SemaphoreType.DMA((2, 2))],
    )
    return pl.pallas_call(
        functools.partial(_cmp_sample_kernel, n_pages=n_pages),
        grid_spec=grid_spec,
        out_shape=[jax.ShapeDtypeStruct((nbatch, nh, KV_W), BF16)] * 2,
        compiler_params=_params(("arbitrary",)),
        name="cmp_sample",
    )(page_table, *cwk, *cwv, cache_k, cache_v)


def _nsa_sample_kernel(pt_ref, cend_ref, qb_ref, gt_ref, ck_ref, cv_ref, ksn_ref, vsn_ref, kwn_ref, vwn_ref,
                       wk_ref, wv_ref, e_ref, ks_hbm, vs_hbm, o_ref, kbuf, vbuf, sems, *, n_pages, past, chunk):
    b = pl.program_id(0)
    nbatch = pl.num_programs(0)
    slot = b % 2
    t = qb_ref.shape[0]
    nrow = NSA_HEADS * t
    nbp = past // SEL_BLOCK
    nbl = e_ref.shape[0]
    streams = ((ks_hbm, _cols_dst(kbuf, PAGE)), (vs_hbm, _cols_dst(vbuf, PAGE)))

    def copies(seq, sl):
        return _page_copies(pt_ref, seq, 0, n_pages, sl, streams, sems)

    @pl.when(b == 0)
    def _():
        for cp in copies(0, 0):
            cp.start()

    @pl.when(b + 1 < nbatch)
    def _():
        for cp in copies(b + 1, 1 - slot):
            cp.start()

    qn = jnp.concatenate([qb_ref[:, h * LANES:(h + 1) * LANES] for h in range(NSA_HEADS)], axis=0).astype(BF16)
    rowq = lax.broadcasted_iota(jnp.int32, (nrow, 1), 0) & (t - 1)
    qpos = past + rowq
    gt = gt_ref[...]

    def gate(br):
        return jnp.concatenate([gt[:, h * N_BRANCH + br:h * N_BRANCH + br + 1] for h in range(NSA_HEADS)], axis=0)

    for cp in copies(b, slot):
        cp.wait()
    n_chunks = past // chunk
    s_cmp = _dot_nt(qn, ck_ref[...])
    wlen = wk_ref.shape[1]
    s_w = _dot(qn, wk_ref[...].astype(BF16))
    s_wn = _dot_nt(qn, _pad_rows(kwn_ref[...], LANES).astype(BF16))
    s_n = _dot_nt(qn, _pad_rows(ksn_ref[...], LANES).astype(BF16))
    s_sel = [_dot(qn, kbuf[slot, :, pl.ds(c * chunk, chunk)].astype(BF16)) for c in range(n_chunks)]
    col = lax.broadcasted_iota(jnp.int32, (nrow, LANES), 1)
    new_ok = (col < t) & (col <= rowq)

    wcol = lax.broadcasted_iota(jnp.int32, (nrow, wlen), 1)
    pw = _softmax_rows(jnp.concatenate([s_w, s_wn], axis=1),
                       jnp.concatenate([wcol > rowq + (wlen - WINDOW), new_ok], axis=1))
    o_win = (_dot_nt(pw[:, :wlen].astype(BF16), wv_ref[...].astype(BF16))
             + _dot(pw[:, wlen:].astype(BF16), _pad_rows(vwn_ref[...], LANES).astype(BF16)))

    p = _softmax_rows(s_cmp, cend_ref[...] <= qpos)
    res = _dot(p.astype(BF16), cv_ref[...]) * gate(0) + o_win * gate(2)
    imps = []
    for g in range(NSA_GROUPS):
        psum = p[g * NSA_HPG * t:g * NSA_HPG * t + t]
        for hh in range(1, NSA_HPG):
            r = (g * NSA_HPG + hh) * t
            psum = psum + p[r:r + t]
        imp = psum[:, 0:nbp]
        for r in range(1, CMP_PER_SEL):
            imp = imp + psum[:, r * nbp:(r + 1) * nbp]
        imps.append(jnp.concatenate([imp, jnp.zeros((t, nbl - nbp), F32)], axis=1))
    q16 = past + (lax.broadcasted_iota(jnp.int32, (NSA_GROUPS * t, 1), 0) & (t - 1))
    sel = _select_blocks(jnp.concatenate(imps, axis=0), q16 // SEL_BLOCK)
    sel_rows = jnp.concatenate([sel[0:t]] * NSA_HPG + [sel[t:2 * t]] * NSA_HPG, axis=0)
    sel_b16 = sel_rows.astype(BF16)

    oks = [_dot(sel_b16, e_ref[:, c * chunk:(c + 1) * chunk]) > 0.5 for c in range(n_chunks)]
    s_sel = [jnp.where(ok, s, NEG) for ok, s in zip(oks, s_sel)]
    s_n = jnp.where(new_ok & (sel_rows[:, nbp:nbp + 1] > 0.5), s_n, NEG)
    m = jnp.max(s_n, axis=-1, keepdims=True)
    for s in s_sel:
        m = jnp.maximum(m, jnp.max(s, axis=-1, keepdims=True))
    p_n = jnp.exp(s_n - m)
    l = jnp.sum(p_n, axis=-1, keepdims=True)
    acc = _dot(p_n.astype(BF16), _pad_rows(vsn_ref[...], LANES).astype(BF16))
    for c in range(n_chunks):
        p_c = jnp.exp(s_sel[c] - m)
        l = l + jnp.sum(p_c, axis=-1, keepdims=True)
        acc = acc + _dot_nt(p_c.astype(BF16), vbuf[slot, :, pl.ds(c * chunk, chunk)].astype(BF16))
    res = res + (acc / l) * gate(1)

    lo = lax.broadcasted_iota(jnp.int32, (t, LANES), 1) < NSA_DIM
    for h in range(NSA_HPG):
        o_ref[:, h * LANES:(h + 1) * LANES] = jnp.where(lo, res[h * t:(h + 1) * t],
                                                        res[(h + NSA_HPG) * t:(h + NSA_HPG + 1) * t])


def _nsa_sample(s, ck, cv, page_table, cache_k, cache_v, win_k, win_v, past, t):
    nbatch, n_pages = page_table.shape
    nh = ck.shape[1]
    nbp = past // SEL_BLOCK
    nbl = -(-(nbp + 1) // LANES) * LANES
    wlen = win_k.shape[2]
    chunk = min(2048, past)
    assert t == 8 and past % chunk == 0 and past % SEL_BLOCK == 0 and wlen <= past and nh == CMP_PER_SEL * nbp
    pp = np.arange(nh)
    cend = ((CMP_PER_SEL * (pp % nbp) + pp // nbp) * CMP_STRIDE + (CMP_BLOCK - 1)).astype(np.int32)
    e = _expand_matrix_t(past, nbl).T
    row = lambda w: pl.BlockSpec((t, w), lambda b, pt: (b, 0))
    per_seq = lambda n, w: pl.BlockSpec((None, n, w), lambda b, pt: (b, 0, 0))
    anyspec = pl.BlockSpec(memory_space=pl.ANY)
    grid_spec = pltpu.PrefetchScalarGridSpec(
        num_scalar_prefetch=1,
        grid=(nbatch,),
        in_specs=[pl.BlockSpec((1, nh), lambda b, pt: (0, 0)), row(NSA_HEADS * LANES), row(LANES),
                  per_seq(nh, KV_W), per_seq(nh, KV_W), row(KV_W), row(KV_W), row(KV_W), row(KV_W),
                  per_seq(KV_W, wlen), per_seq(KV_W, wlen), _vmem_spec(), anyspec, anyspec],
        out_specs=row(NSA_OUT),
        scratch_shapes=[pltpu.VMEM((2, KV_W, past), F32), pltpu.VMEM((2, KV_W, past), F32),
                        pltpu.SemaphoreType.DMA((2, 2))],
    )
    return pl.pallas_call(
        functools.partial(_nsa_sample_kernel, n_pages=n_pages, past=past, chunk=chunk),
        grid_spec=grid_spec,
        out_shape=jax.ShapeDtypeStruct((nbatch * t, NSA_OUT), F32),
        compiler_params=_params(("arbitrary",)),
        name="nsa_sample",
    )(page_table, jnp.asarray(cend).reshape(1, nh), s["qb"], s["gt"], ck, cv, s["ks"], s["vs"], s["kw"], s["vw"],
      win_k, win_v, e, cache_k, cache_v)


def _rms(x, g):
    return x * lax.rsqrt(jnp.mean(x * x, axis=-1, keepdims=True) + EPS) * g


def _finish_kernel(x_ref, oa_ref, ob_ref, gp_ref, gf_ref, gq_ref, wa_ref, wb_ref, wu_ref, wd_ref, y_ref):
    mix = _dot(oa_ref[...].astype(BF16), wa_ref[...]) + _dot(ob_ref[...].astype(BF16), wb_ref[...])
    x1 = x_ref[...] + _rms(mix, gp_ref[...])
    h = _rms(x1, gf_ref[...]).astype(BF16)
    u = jnp.maximum(_dot(h, wu_ref[...]), 0.0)
    f = _dot((u * u).astype(BF16), wd_ref[...])
    y_ref[...] = x1 + _rms(f, gq_ref[...])


def _finish(x2d, oa, ob, g_post, g_ffn_pre, g_ffn_post, wa, wb, wu, wd, tm):
    rows = x2d.shape[0]
    tm = _row_tile(rows, tm)
    row = lambda w: pl.BlockSpec((tm, w), lambda i: (i, 0))
    vec = pl.BlockSpec((1, D_MODEL), lambda i: (0, 0))
    g = lambda a: a.reshape(1, D_MODEL).astype(F32)
    return pl.pallas_call(
        _finish_kernel,
        grid=(rows // tm,),
        in_specs=[row(D_MODEL), row(DA_OUT), row(NSA_OUT), vec, vec, vec] + [_vmem_spec()] * 4,
        out_specs=row(D_MODEL),
        out_shape=jax.ShapeDtypeStruct((rows, D_MODEL), F32),
        compiler_params=_params(("arbitrary",)),
        name="finish",
    )(x2d, oa, ob, g(g_post), g(g_ffn_pre), g(g_ffn_post), wa, wb, wu, wd)


def kernel(x_prompt, x_sample, cache_da_k, cache_da_v, cache_nsa_cmp_k, cache_nsa_cmp_v, cache_nsa_sel_k,
           cache_nsa_sel_v, state_nsa_win_k, state_nsa_win_v, page_table, norm_mix_pre, norm_mix_post,
           norm_ffn_pre, norm_ffn_post, w_in, w_out, da_lambda_q1, da_lambda_k1, da_lambda_q2, da_lambda_k2,
           da_subln, cmp_pe_k, cmp_w1_k, cmp_w2_k, cmp_pe_v, cmp_w1_v, cmp_w2_v, w_up, w_down):
    depth = w_in.shape[0]
    n_p, seq = x_prompt.shape[:2]
    nbatch, t = x_sample.shape[:2]
    n_pool = cache_da_k.shape[1]
    past = page_table.shape[1] * PAGE
    assert n_p == 1 and cache_da_k.shape[2] == PAGE
    xp = x_prompt.reshape(seq, D_MODEL)
    xs = x_sample.reshape(nbatch * t, D_MODEL)
    pos_p = jnp.arange(seq)
    pos_s = past + jnp.tile(jnp.arange(t), nbatch)
    w_keep = min(WINDOW, seq)
    p_rows, s_rows = [], []
    for layer in range(depth):
        lam_init = 0.8 - 0.6 * math.exp(-0.3 * layer)
        lams = [a[layer].reshape(1, DA_DIM).astype(F32)
                for a in (da_lambda_q1, da_lambda_k1, da_lambda_q2, da_lambda_k2)]
        subln = da_subln[layer].reshape(1, LANES).astype(F32)
        wp = _prep_w_in(w_in[layer])
        cwk = _prep_cmp(cmp_pe_k[layer], cmp_w1_k[layer], cmp_w2_k[layer])
        cwv = _prep_cmp(cmp_pe_v[layer], cmp_w1_v[layer], cmp_w2_v[layer])
        wa, wb = _prep_w_out(w_out[layer])
        wu, wd = w_up[layer].astype(BF16), w_down[layer].astype(BF16)
        post = (norm_mix_post[layer], norm_ffn_pre[layer], norm_ffn_post[layer], wa, wb, wu, wd)

        p = _project(xp, pos_p, wp, norm_mix_pre[layer], BF16, 512)
        pt = _project_t(xp, pos_p, _prep_w_in_t(w_in[layer]), norm_mix_pre[layer], 512)
        oa_p = _da_prompt(pt, p["ka_b"], lams, subln, lam_init, 256, 1024, 2)
        ck = _compress_prompt(p["kc"], cwk)
        cv = _compress_prompt(p["vc"], cwv)
        ob_p = _nsa_prompt_t(pt, p["ks_b"], p["kw_b"], ck, cv, 256, 512)
        kv_t = lambda a: jnp.transpose(a.reshape(NSA_GROUPS, NSA_DIM, -1), (2, 0, 1))[None]
        p_rows.append((jnp.transpose(pt["kat"].reshape(DA_HEADS, 2, DA_DIM, seq), (3, 0, 1, 2))[None],
                       p["va"].reshape(1, seq, DA_HEADS, 2 * DA_DIM),
                       kv_t(pt["kct"]), kv_t(pt["vct"]), kv_t(pt["kst"]), kv_t(pt["vst"]),
                       kv_t(pt["kwt"][:, seq - w_keep:]), kv_t(pt["vwt"][:, seq - w_keep:])))

        s = _project(xs, pos_s, wp, norm_mix_pre[layer], F32, 512)
        da_k_t = jnp.transpose(cache_da_k[layer], (0, 2, 3, 4, 1)).reshape(n_pool, DA_OUT, PAGE)
        da_v_r = cache_da_v[layer].reshape(n_pool, PAGE * DA_HEADS, 2 * DA_DIM)
        oa_s = _da_sample(s, page_table, da_k_t, da_v_r, lams, subln, lam_init, t)
        pk_t = lambda c: jnp.transpose(c[layer], (0, 2, 3, 1)).reshape(-1, KV_W, c.shape[2])
        cks, cvs = _compress_sample(page_table, pk_t(cache_nsa_cmp_k), pk_t(cache_nsa_cmp_v), cwk, cwv)
        ob_s = _nsa_sample(s, cks, cvs, page_table, pk_t(cache_nsa_sel_k), pk_t(cache_nsa_sel_v),
                           pk_t(state_nsa_win_k), pk_t(state_nsa_win_v), past, t)
        kvs = lambda a: a.reshape(nbatch, t, NSA_GROUPS, NSA_DIM)
        nwk = jnp.concatenate([state_nsa_win_k[layer], kvs(s["kw"])], axis=1)[:, t:]
        nwv = jnp.concatenate([state_nsa_win_v[layer], kvs(s["vw"])], axis=1)[:, t:]
        s_rows.append((s["ka"].reshape(nbatch, t, DA_HEADS, 2, DA_DIM), s["va"].reshape(nbatch, t, DA_HEADS, 2 * DA_DIM),
                       kvs(s["kc"]), kvs(s["vc"]), kvs(s["ks"]), kvs(s["vs"]), nwk, nwv))

        xp = _finish(xp, oa_p, ob_p, *post, 256)
        xs = _finish(xs, oa_s, ob_s, *post, 256)

    p_out = [jnp.stack(list(c), axis=0) for c in zip(*p_rows)]
    s_out = [jnp.stack(list(c), axis=0) for c in zip(*s_rows)]
    return (xp.reshape(1, seq, D_MODEL), xs.reshape(nbatch, t, D_MODEL), *p_out, *s_out)
```

```python
import functools
import math

import jax
import jax.numpy as jnp
import numpy as np
from jax import lax
from jax.experimental import pallas as pl
from jax.experimental.pallas import tpu as pltpu

F32 = jnp.float32
BF16 = jnp.bfloat16

LANES = 128
D_MODEL = 1024
DA_HEADS = 4
DA_DIM = 64
DA_OUT = DA_HEADS * 2 * DA_DIM
NSA_HEADS = 8
NSA_GROUPS = 2
NSA_HPG = NSA_HEADS // NSA_GROUPS
NSA_DIM = 64
NSA_OUT = NSA_HEADS * NSA_DIM
KV_W = NSA_GROUPS * NSA_DIM
CMP_STRIDE = 16
CMP_BLOCK = 32
CMP_HIDDEN = 128
SEL_BLOCK = 64
CMP_PER_SEL = SEL_BLOCK // CMP_STRIDE
SEL_TOPK = 16
WINDOW = 512
N_BRANCH = 3
ROPE_THETA = 500000.0
ROPE_DIMS = 16
D_FF = 4 * D_MODEL
EPS = 1e-6
NEG = -1e30
PAGE = 128
SCALE = 0.125
LOG2E = math.log2(math.e)
HALF_W = CMP_STRIDE * KV_W

VMEM_LIMIT = 56 * 1024 * 1024

C_QA, C_KA, C_VA, C_QB = 0, 512, 1024, 1536
C_KV = 2560
C_GT = C_KV + 6 * KV_W
N_PROJ = C_GT + LANES


def _dot(a, b):
    return jnp.dot(a, b, preferred_element_type=F32)


def _dot_nt(a, b):
    return lax.dot_general(a, b, (((1,), (1,)), ((), ())), preferred_element_type=F32)


def _params(sem=None):
    return pltpu.CompilerParams(dimension_semantics=sem, vmem_limit_bytes=VMEM_LIMIT)


def _vmem_spec():
    return pl.BlockSpec(memory_space=pltpu.VMEM)


def _row_tile(rows, preferred):
    return preferred if rows % preferred == 0 else rows


def _prep_w_in(w):
    qa, ka, va = w[:, 0:512], w[:, 512:1024], w[:, 1024:1536]
    qb = w[:, 1536:2048].reshape(D_MODEL, NSA_HEADS, NSA_DIM)
    kv6 = w[:, 2048:2816]
    gl = w[:, 2816:2840]
    z = jnp.zeros((D_MODEL, NSA_DIM), w.dtype)
    qb_w = []
    for h in range(NSA_HEADS):
        qb_w += [qb[:, h], z] if h // NSA_HPG == 0 else [z, qb[:, h]]
    gl_p = jnp.concatenate([gl, jnp.zeros((D_MODEL, LANES - gl.shape[1]), w.dtype)], axis=1)
    return jnp.concatenate([qa, ka, va] + qb_w + [kv6, gl_p], axis=1).astype(BF16)


def _prep_cmp(pe, w1, w2):
    def expand(w1h):
        w = w1h.reshape(CMP_STRIDE, NSA_DIM, CMP_HIDDEN)
        z = jnp.zeros_like(w)
        g0 = jnp.concatenate([w, z], axis=-1)
        g1 = jnp.concatenate([z, w], axis=-1)
        return jnp.stack([g0, g1], axis=1).reshape(HALF_W, 2 * CMP_HIDDEN).astype(BF16)

    def tile_pe(p):
        return jnp.tile(p[:, None, :], (1, NSA_GROUPS, 1)).reshape(1, HALF_W).astype(F32)

    n = CMP_STRIDE * NSA_DIM
    z2 = jnp.zeros_like(w2)
    w2e = jnp.concatenate([jnp.concatenate([w2, z2], axis=1),
                           jnp.concatenate([z2, w2], axis=1)], axis=0).astype(BF16)
    return (tile_pe(pe[:CMP_STRIDE]), tile_pe(pe[CMP_STRIDE:]), expand(w1[:n]), expand(w1[n:]), w2e)


def _prep_w_out(w_out):
    wa = w_out[:DA_OUT]
    wb = w_out[DA_OUT:].reshape(NSA_HEADS, NSA_DIM, D_MODEL)
    order = [0, 4, 1, 5, 2, 6, 3, 7]
    wb = jnp.concatenate([wb[h] for h in order], axis=0)
    return wa.astype(BF16), wb.astype(BF16)


def _rope_tables(pos):
    half = ROPE_DIMS // 2
    inv = jnp.power(jnp.float32(ROPE_THETA), -jnp.arange(half, dtype=F32) * (2.0 / ROPE_DIMS))
    ang = pos.astype(F32)[:, None] * inv[None, :]
    cos, sin = jnp.cos(ang), jnp.sin(ang)
    n = pos.shape[0]
    one = jnp.ones((n, NSA_DIM - ROPE_DIMS), F32)
    zero = jnp.zeros((n, NSA_DIM - ROPE_DIMS), F32)
    z8 = jnp.zeros((n, half), F32)
    c = jnp.concatenate([cos, cos, one], axis=1)
    a = jnp.concatenate([-sin, z8, zero], axis=1)
    b = jnp.concatenate([z8, sin, zero], axis=1)
    return tuple(jnp.concatenate([t, t], axis=1) for t in (c, a, b))


def _expand_matrix_t(n_keys, n_blocks):
    k = np.arange(n_keys)[:, None] // SEL_BLOCK
    b = np.arange(n_blocks)[None, :]
    return jnp.asarray((k == b).astype(np.float32), dtype=BF16)


def _proj_kernel(x_ref, c_ref, a_ref, b_ref, g_ref, w_ref,
                 qa0_ref, qa1_ref, qb_ref, ka_ref, va_ref,
                 kc_ref, vc_ref, ks_ref, vs_ref, kw_ref, vw_ref, gt_ref,
                 kab_ref, vab_ref, ksb_ref, vsb_ref, kwb_ref, vwb_ref):
    x = x_ref[...]
    ms = jnp.mean(x * x, axis=-1, keepdims=True)
    h = (x * lax.rsqrt(ms + EPS) * g_ref[...]).astype(BF16)
    c, a, b = c_ref[...], a_ref[...], b_ref[...]

    def rope(z):
        return z * c + pltpu.roll(z, LANES - ROPE_DIMS // 2, 1) * a + pltpu.roll(z, ROPE_DIMS // 2, 1) * b

    def sect(col, width):
        return _dot(h, w_ref[:, col:col + width])

    lane = lax.broadcasted_iota(jnp.int32, (x.shape[0], LANES), 1)
    lo = lane < DA_DIM

    z = sect(C_QA, DA_OUT)
    for i in range(DA_HEADS):
        q = rope(z[:, i * LANES:(i + 1) * LANES]) * SCALE
        qa0_ref[:, i * LANES:(i + 1) * LANES] = jnp.where(lo, q, 0.0).astype(qa0_ref.dtype)
        qa1_ref[:, i * LANES:(i + 1) * LANES] = jnp.where(lo, 0.0, q).astype(qa1_ref.dtype)
    z = sect(C_KA, DA_OUT)
    for i in range(DA_HEADS):
        k = rope(z[:, i * LANES:(i + 1) * LANES])
        ka_ref[:, i * LANES:(i + 1) * LANES] = k
        kab_ref[:, i * LANES:(i + 1) * LANES] = k.astype(BF16)
    z = sect(C_VA, DA_OUT)
    va_ref[...] = z
    vab_ref[...] = z.astype(BF16)
    z = sect(C_QB, NSA_HEADS * LANES)
    for i in range(NSA_HEADS):
        qb_ref[:, i * LANES:(i + 1) * LANES] = (rope(z[:, i * LANES:(i + 1) * LANES]) * SCALE).astype(qb_ref.dtype)
    z = sect(C_KV, 6 * KV_W)
    kc_ref[...] = rope(z[:, 0:KV_W])
    vc_ref[...] = z[:, KV_W:2 * KV_W]
    k = rope(z[:, 2 * KV_W:3 * KV_W])
    ks_ref[...] = k
    ksb_ref[...] = k.astype(BF16)
    v = z[:, 3 * KV_W:4 * KV_W]
    vs_ref[...] = v
    vsb_ref[...] = v.astype(BF16)
    k = rope(z[:, 4 * KV_W:5 * KV_W])
    kw_ref[...] = k
    kwb_ref[...] = k.astype(BF16)
    v = z[:, 5 * KV_W:6 * KV_W]
    vw_ref[...] = v
    vwb_ref[...] = v.astype(BF16)
    gl = sect(C_GT, LANES)
    gt_ref[...] = 1.0 / (1.0 + jnp.exp(-gl))


def _project(x2d, pos, wp, g_pre, q_dtype, tm):
    rows = x2d.shape[0]
    tm = _row_tile(rows, tm)
    tabs = _rope_tables(pos)
    row_blk = lambda w: pl.BlockSpec((tm, w), lambda i: (i, 0))
    full = lambda shp: pl.BlockSpec(shp, lambda i: (0, 0))
    widths_f32 = [DA_OUT, DA_OUT] + [KV_W] * 6 + [LANES]
    widths_b16 = [DA_OUT, DA_OUT] + [KV_W] * 4
    out_shape = ([jax.ShapeDtypeStruct((rows, DA_OUT), q_dtype)] * 2
                 + [jax.ShapeDtypeStruct((rows, NSA_HEADS * LANES), q_dtype)]
                 + [jax.ShapeDtypeStruct((rows, w), F32) for w in widths_f32]
                 + [jax.ShapeDtypeStruct((rows, w), BF16) for w in widths_b16])
    out_specs = ([row_blk(DA_OUT)] * 2 + [row_blk(NSA_HEADS * LANES)]
                 + [row_blk(w) for w in widths_f32] + [row_blk(w) for w in widths_b16])
    outs = pl.pallas_call(
        _proj_kernel,
        grid=(rows // tm,),
        in_specs=[row_blk(D_MODEL), row_blk(LANES), row_blk(LANES), row_blk(LANES),
                  full((1, D_MODEL)), full((D_MODEL, N_PROJ))],
        out_specs=out_specs,
        out_shape=out_shape,
        compiler_params=_params(("arbitrary",)),
        name="proj",
    )(x2d, *tabs, g_pre.reshape(1, D_MODEL).astype(F32), wp)
    names = ("qa0", "qa1", "qb", "ka", "va", "kc", "vc", "ks", "vs", "kw", "vw", "gt",
             "ka_b", "va_b", "ks_b", "vs_b", "kw_b", "vw_b")
    return dict(zip(names, outs))


R_QA, R_QB, R_KA, R_VA, R_KV = 0, 512, 1536, 2048, 2560
R_GT = R_KV + 6 * KV_W
N_GATES = NSA_HEADS * N_BRANCH
N_PROJ_T = R_GT + N_GATES


def _prep_w_in_t(w):
    wp = _prep_w_in(w)
    return jnp.concatenate([wp[:, C_QA:C_KA], wp[:, C_QB:C_KV], wp[:, C_KA:C_QB],
                            wp[:, C_KV:C_GT + N_GATES]], axis=1).T


def _rope_tables_t(pos):
    return tuple(tab.T for tab in _rope_tables(pos))


def _proj_t_kernel(x_ref, c_ref, a_ref, b_ref, g_ref, wt_ref,
                   qa0t_ref, qa1t_ref, qbt_ref, vat_b_ref, vst_b_ref, vwt_b_ref,
                   kat_ref, kct_ref, vct_ref, kst_ref, vst_ref, kwt_ref, vwt_ref, gtt_ref):
    x = x_ref[...]
    ms = jnp.mean(x * x, axis=-1, keepdims=True)
    h = (x * lax.rsqrt(ms + EPS) * g_ref[...]).astype(BF16)
    c, a, b = c_ref[...], a_ref[...], b_ref[...]

    def rope(z):
        return z * c + pltpu.roll(z, LANES - ROPE_DIMS // 2, 0) * a + pltpu.roll(z, ROPE_DIMS // 2, 0) * b

    def sect(r, n):
        return _dot_nt(wt_ref[r:r + n, :], h)

    def grp(z, i):
        return z[i * LANES:(i + 1) * LANES]

    lo = lax.broadcasted_iota(jnp.int32, (LANES, x.shape[0]), 0) < DA_DIM
    z = sect(R_QA, DA_OUT)
    for i in range(DA_HEADS):
        q = rope(grp(z, i)) * (SCALE * LOG2E)
        qa0t_ref[i * LANES:(i + 1) * LANES, :] = jnp.where(lo, q, 0.0).astype(BF16)
        qa1t_ref[i * LANES:(i + 1) * LANES, :] = jnp.where(lo, 0.0, q).astype(BF16)
    z = sect(R_QB, NSA_HEADS * LANES)
    for i in range(NSA_HEADS):
        qbt_ref[i * LANES:(i + 1) * LANES, :] = (rope(grp(z, i)) * (SCALE * LOG2E)).astype(BF16)
    z = sect(R_KA, DA_OUT)
    for i in range(DA_HEADS):
        kat_ref[i * LANES:(i + 1) * LANES, :] = rope(grp(z, i))
    vat_b_ref[...] = sect(R_VA, DA_OUT).astype(BF16)
    z = sect(R_KV, 6 * KV_W)
    kct_ref[...] = rope(grp(z, 0))
    vct_ref[...] = grp(z, 1)
    kst_ref[...] = rope(grp(z, 2))
    vst_ref[...] = grp(z, 3)
    vst_b_ref[...] = grp(z, 3).astype(BF16)
    kwt_ref[...] = rope(grp(z, 4))
    vwt_ref[...] = grp(z, 5)
    vwt_b_ref[...] = grp(z, 5).astype(BF16)
    gtt_ref[...] = 1.0 / (1.0 + jnp.exp(-sect(R_GT, N_GATES)))


def _project_t(x2d, pos, wpt, g_pre, tm):
    rows = x2d.shape[0]
    tm = _row_tile(rows, tm)
    tabs = _rope_tables_t(pos)
    col_blk = lambda n: pl.BlockSpec((n, tm), lambda i: (0, i))
    full = lambda shp: pl.BlockSpec(shp, lambda i: (0, 0))
    heights = ([(DA_OUT, BF16)] * 2 + [(NSA_HEADS * LANES, BF16), (DA_OUT, BF16), (KV_W, BF16), (KV_W, BF16),
                                      (DA_OUT, F32)] + [(KV_W, F32)] * 6 + [(N_GATES, F32)])
    outs = pl.pallas_call(
        _proj_t_kernel,
        grid=(rows // tm,),
        in_specs=[pl.BlockSpec((tm, D_MODEL), lambda i: (i, 0)), col_blk(LANES), col_blk(LANES), col_blk(LANES),
                  full((1, D_MODEL)), full((N_PROJ_T, D_MODEL))],
        out_specs=[col_blk(n) for n, _ in heights],
        out_shape=[jax.ShapeDtypeStruct((n, rows), dt) for n, dt in heights],
        compiler_params=_params(("arbitrary",)),
        name="proj_t",
    )(x2d, *tabs, g_pre.reshape(1, D_MODEL).astype(F32), wpt)
    names = ("qa0t", "qa1t", "qbt", "vat_b", "vst_b", "vwt_b", "kat", "kct", "vct", "kst", "vst", "kwt", "vwt", "gtt")
    return dict(zip(names, outs))


def _compress_halves(rows, pet, peb, w1t, w1b, w2):
    nh = rows.shape[0]
    top = _dot((rows + pet).astype(BF16), w1t)
    bot = _dot((rows + peb).astype(BF16), w1b)
    hid = top + pltpu.roll(bot, nh - 1, 0)
    act = hid * (1.0 / (1.0 + jnp.exp(-hid)))
    return _dot(act.astype(BF16), w2)


def _store_permuted(tok, tmp_ref, out_ref):
    nb = tok.shape[0] // CMP_PER_SEL
    tmp_ref[...] = tok
    for r in range(CMP_PER_SEL):
        out_ref[r * nb:(r + 1) * nb, :] = tmp_ref[pl.ds(r, nb, stride=CMP_PER_SEL), :].astype(out_ref.dtype)


def _cmp_prompt_kernel(rows_ref, pet_ref, peb_ref, w1t_ref, w1b_ref, w2_ref, out_ref, tmp_ref):
    tok = _compress_halves(rows_ref[...], pet_ref[...], peb_ref[...], w1t_ref[...], w1b_ref[...], w2_ref[...])
    _store_permuted(tok, tmp_ref, out_ref)


def _compress_prompt(rows, cw):
    nh = rows.shape[0] // CMP_STRIDE
    return pl.pallas_call(
        _cmp_prompt_kernel,
        in_specs=[_vmem_spec()] * 6,
        out_specs=_vmem_spec(),
        out_shape=jax.ShapeDtypeStruct((nh, KV_W), BF16),
        scratch_shapes=[pltpu.VMEM((nh, KV_W), F32)],
        compiler_params=_params(),
        name="cmp_prompt",
    )(rows.reshape(nh, HALF_W), *cw)


def _lambda(lq1, lk1, lq2, lk2, lam_init):
    return (jnp.exp(jnp.sum(lq1 * lk1, axis=-1, keepdims=True))
            - jnp.exp(jnp.sum(lq2 * lk2, axis=-1, keepdims=True)) + lam_init)


def _subln(o, g, lam_init):
    ms = jnp.mean(o * o, axis=-1, keepdims=True)
    return o * lax.rsqrt(ms + EPS) * g * (1.0 - lam_init)


ONES_ROWS = 16


def _da_prompt_kernel(lq1_ref, lk1_ref, lq2_ref, lk2_ref, sg_ref, q0t_ref, q1t_ref, k_ref, vt_ref, o_ref,
                      *scratch, tq, tk, nh, lam_init):
    m_scs, acc_scs = scratch[:nh], scratch[nh:]
    qi = pl.program_id(1)
    q0 = qi * tq
    lam = _lambda(lq1_ref[...], lk1_ref[...], lq2_ref[...], lk2_ref[...], lam_init)
    qcats = [jnp.concatenate([q0t_ref[hd * LANES:(hd + 1) * LANES, :], q1t_ref[hd * LANES:(hd + 1) * LANES, :]],
                             axis=1) for hd in range(nh)]
    for hd in range(nh):
        m_scs[hd][...] = jnp.full(m_scs[hd].shape, NEG, F32)
        acc_scs[hd][...] = jnp.zeros(acc_scs[hd].shape, F32)

    def tile(start, n, masked):
        items = [(sb, hd) for sb in range(n // tq) for hd in range(nh)]

        def at(i):
            sb, hd = items[i]
            return pl.multiple_of(start + sb * tq, tq), hd

        def score(i):
            s0, hd = at(i)
            return _dot(k_ref[pl.ds(s0, tq), hd * LANES:(hd + 1) * LANES], qcats[hd])

        def probs(i, st):
            s0, hd = at(i)
            if masked:
                kpos = s0 + lax.broadcasted_iota(jnp.int32, st.shape, 0)
                qpos = q0 + (lax.broadcasted_iota(jnp.int32, st.shape, 1) & (tq - 1))
                st = jnp.where(kpos <= qpos, st, NEG)
            m_old = m_scs[hd][...]
            m_new = jnp.maximum(m_old, jnp.max(st, axis=0, keepdims=True))
            m_scs[hd][...] = m_new
            return jnp.exp2(st - m_new[0:1]).astype(BF16), jnp.exp2(m_old - m_new)

        def pv(i, pt):
            s0, hd = at(i)
            vt = jnp.concatenate([vt_ref[hd * LANES:(hd + 1) * LANES, pl.ds(s0, tq)],
                                  jnp.ones((ONES_ROWS, tq), BF16)], axis=0)
            return _dot(vt, pt)

        def finish(i, alpha, o):
            hd = items[i][1]
            acc_scs[hd][...] = acc_scs[hd][...] * alpha[0:1] + o

        _staggered(len(items), score, probs, pv, finish)

    def body(kj, carry):
        tile(pl.multiple_of(kj * tk, tk), tk, False)
        return carry

    n_big = q0 // tk
    lax.fori_loop(0, n_big, body, 0)
    for d in range(tk // tq - 1):
        @pl.when(q0 - n_big * tk > d * tq)
        def _():
            tile(pl.multiple_of(n_big * tk + d * tq, tq), tq, False)
    tile(pl.multiple_of(q0, tq), tq, True)
    for hd in range(nh):
        a0, a1 = acc_scs[hd][:, :tq], acc_scs[hd][:, tq:]
        ot = a0[:LANES] / a0[LANES:LANES + 1] - lam * (a1[:LANES] / a1[LANES:LANES + 1])
        ms = jnp.mean(ot * ot, axis=0, keepdims=True)
        yt = ot * lax.rsqrt(ms + EPS) * sg_ref[...] * (1.0 - lam_init)
        o_ref[:, hd * LANES:(hd + 1) * LANES] = yt.T


def _da_prompt(pt, ka_b, lams, subln, lam_init, tq, tk, nh):
    seq = ka_b.shape[0]
    assert seq % tq == 0 and tk % tq == 0 and tq & (tq - 1) == 0 and DA_HEADS % nh == 0
    w = nh * LANES
    vec = lambda: pl.BlockSpec((1, DA_DIM), lambda h, i: (0, 0))
    qblk = pl.BlockSpec((w, tq), lambda h, i: (h, i))
    sg = jnp.broadcast_to(subln.reshape(LANES, 1), (LANES, tq))
    return pl.pallas_call(
        functools.partial(_da_prompt_kernel, tq=tq, tk=tk, nh=nh, lam_init=lam_init),
        grid=(DA_HEADS // nh, seq // tq),
        in_specs=[vec(), vec(), vec(), vec(), pl.BlockSpec((LANES, tq), lambda h, i: (0, 0)),
                  qblk, qblk, pl.BlockSpec((seq, w), lambda h, i: (0, h)),
                  pl.BlockSpec((w, seq), lambda h, i: (h, 0))],
        out_specs=pl.BlockSpec((tq, w), lambda h, i: (i, h)),
        out_shape=jax.ShapeDtypeStruct((seq, DA_OUT), F32),
        scratch_shapes=([pltpu.VMEM((8, 2 * tq), F32)] * nh
                        + [pltpu.VMEM((LANES + ONES_ROWS, 2 * tq), F32)] * nh),
        compiler_params=_params(("arbitrary", "arbitrary")),
        name="da_prompt",
    )(*lams, sg, pt["qa0t"], pt["qa1t"], ka_b, pt["vat_b"])


def _softmax_rows(s, mask):
    s = jnp.where(mask, s, NEG)
    m = jnp.max(s, axis=-1, keepdims=True)
    e = jnp.where(mask, jnp.exp(s - m), 0.0)
    l = jnp.sum(e, axis=-1, keepdims=True)
    return e * (1.0 / jnp.maximum(l, 1e-30))


def _select_blocks(imp, qblk):
    nb = imp.shape[1]
    jb = lax.broadcasted_iota(jnp.int32, imp.shape, 1)
    jbf = jb.astype(F32)
    forced = (jb == 0) | (jb == qblk) | (jb == qblk - 1)
    valid = jb <= qblk
    score = jnp.where(forced, 1e30, jnp.where(valid, imp, -1.0))
    picked = jnp.zeros(imp.shape, F32)
    for _ in range(min(SEL_TOPK, nb)):
        mx = jnp.max(score, axis=-1, keepdims=True)
        first = jnp.min(jnp.where(score == mx, jbf, 1e9), axis=-1, keepdims=True)
        hit = jbf == first
        picked = jnp.where(hit, 1.0, picked)
        score = jnp.where(hit, -2.0, score)
    return jnp.where(valid, picked, 0.0)


def _softmax_cols(st, mask):
    st = jnp.where(mask, st, NEG)
    m = jnp.max(st, axis=0, keepdims=True)
    e = jnp.where(mask, jnp.exp2(st - m), 0.0)
    l = jnp.sum(e, axis=0, keepdims=True)
    return e * (1.0 / jnp.maximum(l, 1e-30))


def _select_blocks_t(imp_t, qblk):
    nb = imp_t.shape[0]
    jb = lax.broadcasted_iota(jnp.int32, imp_t.shape, 0)
    jbf = jb.astype(F32)
    forced = (jb == 0) | (jb == qblk) | (jb == qblk - 1)
    valid = jb <= qblk
    score = jnp.where(forced, 1e30, jnp.where(valid, imp_t, -1.0))
    picked = jnp.zeros(imp_t.shape, F32)
    for _ in range(min(SEL_TOPK, nb)):
        mx = jnp.max(score, axis=0, keepdims=True)
        first = jnp.min(jnp.where(score == mx, jbf, 1e9), axis=0, keepdims=True)
        hit = jbf == first
        picked = jnp.where(hit, 1.0, picked)
        score = jnp.where(hit, -2.0, score)
    return jnp.where(valid, picked, 0.0)


def _staggered(n, score, probs, pv, finish):
    s_next = score(0)
    pending = None
    for i in range(n):
        s_cur = s_next
        if i + 1 < n:
            s_next = score(i + 1)
        p, aux = probs(i, s_cur)
        out = pv(i, p)
        if pending is not None:
            finish(*pending)
        pending = (i, aux, out)
    finish(*pending)


def _nsa_prompt_t_kernel(cend_ref, qt_ref, gt_ref, ck_ref, cvt_ref, ks_ref, vst_ref, kw_ref, vwt_ref, o_ref,
                         sel_sc, res_sc, *scratch, tq, tk):
    m_scs, acc_scs = scratch[:NSA_HEADS], scratch[NSA_HEADS:]
    qi = pl.program_id(0)
    q0 = qi * tq
    nb = sel_sc.shape[1]
    qpos = q0 + lax.broadcasted_iota(jnp.int32, (1, tq), 1)

    def q_head(h):
        return qt_ref[h * LANES:(h + 1) * LANES, :]

    def gate(h, br):
        i = h * N_BRANCH + br
        return gt_ref[i:i + 1, :]

    def with_ones(vt):
        return jnp.concatenate([vt, jnp.ones((ONES_ROWS, vt.shape[1]), BF16)], axis=0)

    ck = ck_ref[...]
    cvt = cvt_ref[...]
    cmask = cend_ref[...] <= qpos
    psums = [None] * NSA_GROUPS

    def cmp_probs(h, st):
        p = _softmax_cols(st, cmask)
        g = h // NSA_HPG
        psums[g] = p if psums[g] is None else psums[g] + p
        return p.astype(BF16), None

    def cmp_finish(h, _, o):
        res_sc[h] = o * gate(h, 0)

    _staggered(NSA_HEADS, lambda h: _dot(ck, q_head(h)), cmp_probs, lambda h, p: _dot(cvt, p), cmp_finish)
    imps = []
    for g in range(NSA_GROUPS):
        imp = psums[g][0:nb]
        for r in range(1, CMP_PER_SEL):
            imp = imp + psums[g][r * nb:(r + 1) * nb]
        imps.append(imp)
    sel = _select_blocks_t(jnp.concatenate(imps, axis=1), jnp.concatenate([qpos, qpos], axis=1) // SEL_BLOCK)
    for g in range(NSA_GROUPS):
        sel_sc[g] = sel[:, g * tq:(g + 1) * tq]

    for h in range(NSA_HEADS):
        m_scs[h][...] = jnp.full(m_scs[h].shape, NEG, F32)
        acc_scs[h][...] = jnp.zeros(acc_scs[h].shape, F32)

    def tile(start, n, causal, sub):
        items = [(sb, h) for sb in range(n // sub) for h in range(NSA_HEADS)]
        masks = {}

        def s0(sb):
            return pl.multiple_of(start + sb * sub, sub)

        def allowed(sb, g):
            if (sb, g) not in masks:
                b0 = s0(sb) // SEL_BLOCK
                rows = [jnp.broadcast_to(sel_sc[g, pl.ds(b0 + i, 1), :], (SEL_BLOCK, tq))
                        for i in range(sub // SEL_BLOCK)]
                ok = jnp.concatenate(rows, axis=0) > 0.5
                if causal:
                    kpos = s0(sb) + lax.broadcasted_iota(jnp.int32, (sub, tq), 0)
                    ok = ok & (kpos <= qpos)
                masks[(sb, g)] = ok
            return masks[(sb, g)]

        def score(i):
            sb, h = items[i]
            return _dot(ks_ref[pl.ds(s0(sb), sub), :], q_head(h))

        def sel_probs(i, st):
            sb, h = items[i]
            st = jnp.where(allowed(sb, h // NSA_HPG), st, NEG)
            m_old = m_scs[h][...]
            m_new = jnp.maximum(m_old, jnp.max(st, axis=0, keepdims=True))
            m_scs[h][...] = m_new
            return jnp.exp2(st - m_new[0:1]).astype(BF16), jnp.exp2(m_old - m_new)

        def sel_pv(i, p):
            return _dot(with_ones(vst_ref[:, pl.ds(s0(items[i][0]), sub)]), p)

        def sel_finish(i, alpha, o):
            h = items[i][1]
            acc_scs[h][...] = acc_scs[h][...] * alpha[0:1] + o

        _staggered(len(items), score, sel_probs, sel_pv, sel_finish)

    def body(kj, carry):
        tile(pl.multiple_of(kj * tk, tk), tk, False, min(tk, 2 * tq))
        return carry

    n_big = q0 // tk
    lax.fori_loop(0, n_big, body, 0)
    for d in range(tk // tq - 1):
        @pl.when(q0 - n_big * tk > d * tq)
        def _():
            tile(pl.multiple_of(n_big * tk + d * tq, tq), tq, False, tq)
    tile(pl.multiple_of(q0, tq), tq, True, tq)

    wlen = WINDOW + tq
    wstart = pl.multiple_of(jnp.maximum(q0 - WINDOW, 0), tq)
    kw = kw_ref[pl.ds(wstart, wlen), :]
    vwt = with_ones(vwt_ref[:, pl.ds(wstart, wlen)])
    kpos = wstart + lax.broadcasted_iota(jnp.int32, (wlen, tq), 0)
    wmask = (kpos <= qpos) & (kpos > qpos - WINDOW)
    def win_probs(h, sw):
        sw = jnp.where(wmask, sw, NEG)
        return jnp.exp2(sw - jnp.max(sw, axis=0, keepdims=True)).astype(BF16), None

    def win_finish(h, _, ow):
        acc = acc_scs[h][...]
        res_sc[h] = (res_sc[h] + (acc[:LANES] / acc[LANES:LANES + 1]) * gate(h, 1)
                     + (ow[:LANES] / ow[LANES:LANES + 1]) * gate(h, 2))

    _staggered(NSA_HEADS, lambda h: _dot(kw, q_head(h)), win_probs, lambda h, p: _dot(vwt, p), win_finish)
    lo = lax.broadcasted_iota(jnp.int32, (LANES, tq), 0) < NSA_DIM
    for h in range(NSA_HPG):
        o_ref[:, h * LANES:(h + 1) * LANES] = jnp.where(lo, res_sc[h], res_sc[h + NSA_HPG]).T


def _nsa_prompt_t(pt, ks_b, kw_b, ck, cv, tq, tk):
    seq = ks_b.shape[0]
    nc = seq // CMP_STRIDE
    nb = seq // SEL_BLOCK
    assert seq % tq == 0 and tk % tq == 0 and WINDOW % tq == 0 and seq >= WINDOW + tq and tq % SEL_BLOCK == 0
    pp = np.arange(nc)
    cend = ((CMP_PER_SEL * (pp % nb) + pp // nb) * CMP_STRIDE + (CMP_BLOCK - 1)).astype(np.int32)
    cend = jnp.asarray(np.broadcast_to(cend[:, None], (nc, tq)))
    return pl.pallas_call(
        functools.partial(_nsa_prompt_t_kernel, tq=tq, tk=tk),
        grid=(seq // tq,),
        in_specs=[_vmem_spec(),
                  pl.BlockSpec((NSA_HEADS * LANES, tq), lambda i: (0, i)),
                  pl.BlockSpec((N_GATES, tq), lambda i: (0, i))] + [_vmem_spec()] * 6,
        out_specs=pl.BlockSpec((tq, NSA_OUT), lambda i: (i, 0)),
        out_shape=jax.ShapeDtypeStruct((seq, NSA_OUT), F32),
        scratch_shapes=([pltpu.VMEM((NSA_GROUPS, nb, tq), F32), pltpu.VMEM((NSA_HEADS, LANES, tq), F32)]
                        + [pltpu.VMEM((8, tq), F32)] * NSA_HEADS
                        + [pltpu.VMEM((LANES + ONES_ROWS, tq), F32)] * NSA_HEADS),
        compiler_params=_params(("arbitrary",)),
        name="nsa_prompt",
    )(cend, pt["qbt"], pt["gtt"], ck, cv.T, ks_b, pt["vst_b"], kw_b, pt["vwt_b"])


def _page_copies(pt_ref, seq, first_page, n_pages, slot, streams, sems):
    out = []
    for j in range(n_pages):
        pg = pt_ref[seq, first_page + j]
        for i, (cache, dst) in enumerate(streams):
            out.append(pltpu.make_async_copy(cache.at[pg], dst(slot, j), sems.at[i, slot]))
    return out


def _rows_dst(buf, rows):
    return lambda slot, j: buf.at[slot, pl.ds(j * rows, rows)]


def _cols_dst(buf, cols):
    return lambda slot, j: buf.at[slot, :, pl.ds(j * cols, cols)]


def _online_update(state, s, pv):
    m_old, l_old, acc = state
    m_new = jnp.maximum(m_old, jnp.max(s, axis=-1, keepdims=True))
    alpha = jnp.exp(m_old - m_new)
    p = jnp.exp(s - m_new)
    l_new = alpha * l_old + jnp.sum(p, axis=-1, keepdims=True)
    return m_new, l_new, alpha * acc + pv(p.astype(BF16))


def _pad_rows(x, rows):
    return jnp.concatenate([x, jnp.zeros((rows - x.shape[0], x.shape[1]), x.dtype)], axis=0)


def _da_sample_kernel(pt_ref, lq1_ref, lk1_ref, lq2_ref, lk2_ref, sg_ref, q0_ref, q1_ref, kn_ref, vn_ref,
                      kc_hbm, vc_hbm, o_ref, kbuf, vbuf, sems, *, ch, nch, lam_init):
    b = pl.program_id(0)
    nbatch = pl.num_programs(0)
    t = q0_ref.shape[0]
    lam = _lambda(lq1_ref[...], lk1_ref[...], lq2_ref[...], lk2_ref[...], lam_init)

    streams = ((kc_hbm, _cols_dst(kbuf, PAGE)), (vc_hbm, _rows_dst(vbuf, PAGE * DA_HEADS)))

    def copies(seq, c, slot):
        return _page_copies(pt_ref, seq, c * ch, ch, slot, streams, sems)

    @pl.when(b == 0)
    def _():
        for cp in copies(0, 0, 0):
            cp.start()

    lane = lax.broadcasted_iota(jnp.int32, (t, DA_OUT), 1)
    parts = []
    for h in range(DA_HEADS):
        in_head = (lane >= h * LANES) & (lane < (h + 1) * LANES)
        parts += [jnp.where(in_head, q0_ref[...], 0.0), jnp.where(in_head, q1_ref[...], 0.0)]
    qbd = jnp.concatenate(parts, axis=0).astype(BF16)
    nrow = qbd.shape[0]
    hrows = 2 * t

    state = (jnp.full((nrow, 1), NEG, F32), jnp.zeros((nrow, 1), F32), jnp.zeros((nrow, LANES), F32))
    for c in range(nch):
        slot = c % 2
        if c + 1 < nch:
            for cp in copies(b, c + 1, (c + 1) % 2):
                cp.start()
        else:
            @pl.when(b + 1 < nbatch)
            def _():
                for cp in copies(b + 1, 0, 0):
                    cp.start()
        for cp in copies(b, c, slot):
            cp.wait()

        def pv(p, slot=slot):
            return jnp.concatenate(
                [_dot(p[h * hrows:(h + 1) * hrows],
                      vbuf[slot, pl.ds(h, ch * PAGE, stride=DA_HEADS), :].astype(BF16))
                 for h in range(DA_HEADS)], axis=0)

        state = _online_update(state, _dot(qbd, kbuf[slot].astype(BF16)), pv)

    kn = _pad_rows(kn_ref[...], LANES).astype(BF16)
    vn = _pad_rows(vn_ref[...], LANES).astype(BF16)
    col = lax.broadcasted_iota(jnp.int32, (nrow, LANES), 1)
    rowq = lax.broadcasted_iota(jnp.int32, (nrow, LANES), 0) & (t - 1)
    s_n = jnp.where((col < t) & (col <= rowq), _dot_nt(qbd, kn), NEG)

    def pv_new(p):
        full = _dot(p, vn)
        return jnp.concatenate([full[h * hrows:(h + 1) * hrows, h * LANES:(h + 1) * LANES]
                                for h in range(DA_HEADS)], axis=0)

    _, l, acc = _online_update(state, s_n, pv_new)
    out = acc / l
    for h in range(DA_HEADS):
        r = h * hrows
        o = out[r:r + t] - lam * out[r + t:r + 2 * t]
        o_ref[:, h * LANES:(h + 1) * LANES] = _subln(o, sg_ref[...], lam_init)


def _da_sample(s, page_table, cache_k, cache_v, lams, subln, lam_init, t):
    nbatch, n_pages = page_table.shape
    assert t == 8 and n_pages % 2 == 0
    ch = min(16, n_pages // 2)
    nch = n_pages // ch
    assert nch % 2 == 0 and nch * ch == n_pages
    vec = lambda w: pl.BlockSpec((1, w), lambda b, pt: (0, 0))
    blk = pl.BlockSpec((t, DA_OUT), lambda b, pt: (b, 0))
    anyspec = pl.BlockSpec(memory_space=pl.ANY)
    grid_spec = pltpu.PrefetchScalarGridSpec(
        num_scalar_prefetch=1,
        grid=(nbatch,),
        in_specs=[vec(DA_DIM)] * 4 + [vec(LANES), blk, blk, blk, blk, anyspec, anyspec],
        out_specs=blk,
        scratch_shapes=[pltpu.VMEM((2, DA_OUT, ch * PAGE), F32), pltpu.VMEM((2, ch * PAGE * DA_HEADS, LANES), F32),
                        pltpu.SemaphoreType.DMA((2, 2))],
    )
    return pl.pallas_call(
        functools.partial(_da_sample_kernel, ch=ch, nch=nch, lam_init=lam_init),
        grid_spec=grid_spec,
        out_shape=jax.ShapeDtypeStruct((nbatch * t, DA_OUT), F32),
        compiler_params=_params(("arbitrary",)),
        name="da_sample",
    )(page_table, *lams, subln, s["qa0"], s["qa1"], s["ka"], s["va"], cache_k, cache_v)


def _cmp_sample_kernel(pt_ref, petk_ref, pebk_ref, w1tk_ref, w1bk_ref, w2k_ref,
                       petv_ref, pebv_ref, w1tv_ref, w1bv_ref, w2v_ref, kc_hbm, vc_hbm,
                       ck_ref, cv_ref, kbuf, vbuf, krows_ref, vrows_ref, tmp_ref, sems, *, n_pages):
    b = pl.program_id(0)
    nbatch = pl.num_programs(0)
    slot = b % 2
    page_dst = lambda buf: (lambda sl, j: buf.at[sl, j])
    streams = ((kc_hbm, page_dst(kbuf)), (vc_hbm, page_dst(vbuf)))

    def copies(seq, sl):
        return _page_copies(pt_ref, seq, 0, n_pages, sl, streams, sems)

    @pl.when(b == 0)
    def _():
        for cp in copies(0, 0):
            cp.start()

    @pl.when(b + 1 < nbatch)
    def _():
        for cp in copies(b + 1, 1 - slot):
            cp.start()

    for cp in copies(b, slot):
        cp.wait()

    def to_rows(buf, rows_ref, pages):
        for j in pages:
            rows_ref[j * PAGE:(j + 1) * PAGE, :] = buf[slot, j].T

    def compress(rows_ref, pet_ref, peb_ref, w1t_ref, w1b_ref, w2_ref, out_ref, between):
        nh = rows_ref.shape[0] // CMP_STRIDE
        top = jnp.zeros((nh, 2 * CMP_HIDDEN), F32)
        bot = jnp.zeros((nh, 2 * CMP_HIDDEN), F32)
        for j in range(0, CMP_STRIDE, 2):
            x = jnp.concatenate([rows_ref[pl.ds(j, nh, stride=CMP_STRIDE), :],
                                 rows_ref[pl.ds(j + 1, nh, stride=CMP_STRIDE), :]], axis=1)
            c0, c1 = j * KV_W, (j + 2) * KV_W
            top = top + _dot((x + pet_ref[:, c0:c1]).astype(BF16), w1t_ref[c0:c1, :])
            bot = bot + _dot((x + peb_ref[:, c0:c1]).astype(BF16), w1b_ref[c0:c1, :])
            between(j // 2)
        hid = top + pltpu.roll(bot, nh - 1, 0)
        act = hid * (1.0 / (1.0 + jnp.exp(-hid)))
        _store_permuted(_dot(act.astype(BF16), w2_ref[...]), tmp_ref, out_ref)

    steps = CMP_STRIDE // 2
    per_step = -(-n_pages // steps)
    to_rows(kbuf, krows_ref, range(n_pages))
    compress(krows_ref, petk_ref, pebk_ref, w1tk_ref, w1bk_ref, w2k_ref, ck_ref,
             lambda i: to_rows(vbuf, vrows_ref, range(min(i * per_step, n_pages), min((i + 1) * per_step, n_pages))))
    compress(vrows_ref, petv_ref, pebv_ref, w1tv_ref, w1bv_ref, w2v_ref, cv_ref, lambda i: None)


def _compress_sample(page_table, cache_k, cache_v, cwk, cwv):
    nbatch, n_pages = page_table.shape
    nh = n_pages * PAGE // CMP_STRIDE
    out_blk = pl.BlockSpec((None, nh, KV_W), lambda b, pt: (b, 0, 0))
    anyspec = pl.BlockSpec(memory_space=pl.ANY)
    grid_spec = pltpu.PrefetchScalarGridSpec(
        num_scalar_prefetch=1,
        grid=(nbatch,),
        in_specs=[_vmem_spec()] * 10 + [anyspec, anyspec],
        out_specs=[out_blk, out_blk],
        scratch_shapes=[pltpu.VMEM((2, n_pages, KV_W, PAGE), F32), pltpu.VMEM((2, n_pages, KV_W, PAGE), F32),
                        pltpu.VMEM((n_pages * PAGE, KV_W), F32), pltpu.VMEM((n_pages * PAGE, KV_W), F32),
                        pltpu.VMEM((nh, KV_W), F32), pltpu.SemaphoreType.DMA((2, 2))],
    )
    return pl.pallas_call(
        functools.partial(_cmp_sample_kernel, n_pages=n_pages),
        grid_spec=grid_spec,
        out_shape=[jax.ShapeDtypeStruct((nbatch, nh, KV_W), BF16)] * 2,
        compiler_params=_params(("arbitrary",)),
        name="cmp_sample",
    )(page_table, *cwk, *cwv, cache_k, cache_v)


def _nsa_sample_kernel(pt_ref, cend_ref, qb_ref, gt_ref, ck_ref, cv_ref, ksn_ref, vsn_ref, kwn_ref, vwn_ref,
                       wk_ref, wv_ref, e_ref, ks_hbm, vs_hbm, o_ref, kbuf, vbuf, sems, *, n_pages, past, chunk):
    b = pl.program_id(0)
    nbatch = pl.num_programs(0)
    slot = b % 2
    t = qb_ref.shape[0]
    nrow = NSA_HEADS * t
    nbp = past // SEL_BLOCK
    nbl = e_ref.shape[0]
    streams = ((ks_hbm, _cols_dst(kbuf, PAGE)), (vs_hbm, _cols_dst(vbuf, PAGE)))

    def copies(seq, sl):
        return _page_copies(pt_ref, seq, 0, n_pages, sl, streams, sems)

    @pl.when(b == 0)
    def _():
        for cp in copies(0, 0):
            cp.start()

    @pl.when(b + 1 < nbatch)
    def _():
        for cp in copies(b + 1, 1 - slot):
            cp.start()

    qn = jnp.concatenate([qb_ref[:, h * LANES:(h + 1) * LANES] for h in range(NSA_HEADS)], axis=0).astype(BF16)
    rowq = lax.broadcasted_iota(jnp.int32, (nrow, 1), 0) & (t - 1)
    qpos = past + rowq
    gt = gt_ref[...]

    def gate(br):
        return jnp.concatenate([gt[:, h * N_BRANCH + br:h * N_BRANCH + br + 1] for h in range(NSA_HEADS)], axis=0)

    for cp in copies(b, slot):
        cp.wait()
    n_chunks = past // chunk
    s_cmp = _dot_nt(qn, ck_ref[...])
    wlen = wk_ref.shape[1]
    s_w = _dot(qn, wk_ref[...].astype(BF16))
    s_wn = _dot_nt(qn, _pad_rows(kwn_ref[...], LANES).astype(BF16))
    s_n = _dot_nt(qn, _pad_rows(ksn_ref[...], LANES).astype(BF16))
    s_sel = [_dot(qn, kbuf[slot, :, pl.ds(c * chunk, chunk)].astype(BF16)) for c in range(n_chunks)]
    col = lax.broadcasted_iota(jnp.int32, (nrow, LANES), 1)
    new_ok = (col < t) & (col <= rowq)

    wcol = lax.broadcasted_iota(jnp.int32, (nrow, wlen), 1)
    pw = _softmax_rows(jnp.concatenate([s_w, s_wn], axis=1),
                       jnp.concatenate([wcol > rowq + (wlen - WINDOW), new_ok], axis=1))
    o_win = (_dot_nt(pw[:, :wlen].astype(BF16), wv_ref[...].astype(BF16))
             + _dot(pw[:, wlen:].astype(BF16), _pad_rows(vwn_ref[...], LANES).astype(BF16)))

    p = _softmax_rows(s_cmp, cend_ref[...] <= qpos)
    res = _dot(p.astype(BF16), cv_ref[...]) * gate(0) + o_win * gate(2)
    imps = []
    for g in range(NSA_GROUPS):
        psum = p[g * NSA_HPG * t:g * NSA_HPG * t + t]
        for hh in range(1, NSA_HPG):
            r = (g * NSA_HPG + hh) * t
            psum = psum + p[r:r + t]
        imp = psum[:, 0:nbp]
        for r in range(1, CMP_PER_SEL):
            imp = imp + psum[:, r * nbp:(r + 1) * nbp]
        imps.append(jnp.concatenate([imp, jnp.zeros((t, nbl - nbp), F32)], axis=1))
    q16 = past + (lax.broadcasted_iota(jnp.int32, (NSA_GROUPS * t, 1), 0) & (t - 1))
    sel = _select_blocks(jnp.concatenate(imps, axis=0), q16 // SEL_BLOCK)
    sel_rows = jnp.concatenate([sel[0:t]] * NSA_HPG + [sel[t:2 * t]] * NSA_HPG, axis=0)
    sel_b16 = sel_rows.astype(BF16)

    oks = [_dot(sel_b16, e_ref[:, c * chunk:(c + 1) * chunk]) > 0.5 for c in range(n_chunks)]
    s_sel = [jnp.where(ok, s, NEG) for ok, s in zip(oks, s_sel)]
    s_n = jnp.where(new_ok & (sel_rows[:, nbp:nbp + 1] > 0.5), s_n, NEG)
    m = jnp.max(s_n, axis=-1, keepdims=True)
    for s in s_sel:
        m = jnp.maximum(m, jnp.max(s, axis=-1, keepdims=True))
    p_n = jnp.exp(s_n - m)
    l = jnp.sum(p_n, axis=-1, keepdims=True)
    acc = _dot(p_n.astype(BF16), _pad_rows(vsn_ref[...], LANES).astype(BF16))
    for c in range(n_chunks):
        p_c = jnp.exp(s_sel[c] - m)
        l = l + jnp.sum(p_c, axis=-1, keepdims=True)
        acc = acc + _dot_nt(p_c.astype(BF16), vbuf[slot, :, pl.ds(c * chunk, chunk)].astype(BF16))
    res = res + (acc / l) * gate(1)

    lo = lax.broadcasted_iota(jnp.int32, (t, LANES), 1) < NSA_DIM
    for h in range(NSA_HPG):
        o_ref[:, h * LANES:(h + 1) * LANES] = jnp.where(lo, res[h * t:(h + 1) * t],
                                                        res[(h + NSA_HPG) * t:(h + NSA_HPG + 1) * t])


def _nsa_sample(s, ck, cv, page_table, cache_k, cache_v, win_k, win_v, past, t):
    nbatch, n_pages = page_table.shape
    nh = ck.shape[1]
    nbp = past // SEL_BLOCK
    nbl = -(-(nbp + 1) // LANES) * LANES
    wlen = win_k.shape[2]
    chunk = min(2048, past)
    assert t == 8 and past % chunk == 0 and past % SEL_BLOCK == 0 and wlen <= past and nh == CMP_PER_SEL * nbp
    pp = np.arange(nh)
    cend = ((CMP_PER_SEL * (pp % nbp) + pp // nbp) * CMP_STRIDE + (CMP_BLOCK - 1)).astype(np.int32)
    e = _expand_matrix_t(past, nbl).T
    row = lambda w: pl.BlockSpec((t, w), lambda b, pt: (b, 0))
    per_seq = lambda n, w: pl.BlockSpec((None, n, w), lambda b, pt: (b, 0, 0))
    anyspec = pl.BlockSpec(memory_space=pl.ANY)
    grid_spec = pltpu.PrefetchScalarGridSpec(
        num_scalar_prefetch=1,
        grid=(nbatch,),
        in_specs=[pl.BlockSpec((1, nh), lambda b, pt: (0, 0)), row(NSA_HEADS * LANES), row(LANES),
                  per_seq(nh, KV_W), per_seq(nh, KV_W), row(KV_W), row(KV_W), row(KV_W), row(KV_W),
                  per_seq(KV_W, wlen), per_seq(KV_W, wlen), _vmem_spec(), anyspec, anyspec],
        out_specs=row(NSA_OUT),
        scratch_shapes=[pltpu.VMEM((2, KV_W, past), F32), pltpu.VMEM((2, KV_W, past), F32),
                        pltpu.SemaphoreType.DMA((2, 2))],
    )
    return pl.pallas_call(
        functools.partial(_nsa_sample_kernel, n_pages=n_pages, past=past, chunk=chunk),
        grid_spec=grid_spec,
        out_shape=jax.ShapeDtypeStruct((nbatch * t, NSA_OUT), F32),
        compiler_params=_params(("arbitrary",)),
        name="nsa_sample",
    )(page_table, jnp.asarray(cend).reshape(1, nh), s["qb"], s["gt"], ck, cv, s["ks"], s["vs"], s["kw"], s["vw"],
      win_k, win_v, e, cache_k, cache_v)


def _rms(x, g):
    return x * lax.rsqrt(jnp.mean(x * x, axis=-1, keepdims=True) + EPS) * g


def _finish_kernel(x_ref, oa_ref, ob_ref, gp_ref, gf_ref, gq_ref, wa_ref, wb_ref, wu_ref, wd_ref, y_ref):
    mix = _dot(oa_ref[...].astype(BF16), wa_ref[...]) + _dot(ob_ref[...].astype(BF16), wb_ref[...])
    x1 = x_ref[...] + _rms(mix, gp_ref[...])
    h = _rms(x1, gf_ref[...]).astype(BF16)
    u = jnp.maximum(_dot(h, wu_ref[...]), 0.0)
    f = _dot((u * u).astype(BF16), wd_ref[...])
    y_ref[...] = x1 + _rms(f, gq_ref[...])


def _finish(x2d, oa, ob, g_post, g_ffn_pre, g_ffn_post, wa, wb, wu, wd, tm):
    rows = x2d.shape[0]
    tm = _row_tile(rows, tm)
    row = lambda w: pl.BlockSpec((tm, w), lambda i: (i, 0))
    vec = pl.BlockSpec((1, D_MODEL), lambda i: (0, 0))
    g = lambda a: a.reshape(1, D_MODEL).astype(F32)
    return pl.pallas_call(
        _finish_kernel,
        grid=(rows // tm,),
        in_specs=[row(D_MODEL), row(DA_OUT), row(NSA_OUT), vec, vec, vec] + [_vmem_spec()] * 4,
        out_specs=row(D_MODEL),
        out_shape=jax.ShapeDtypeStruct((rows, D_MODEL), F32),
        compiler_params=_params(("arbitrary",)),
        name="finish",
    )(x2d, oa, ob, g(g_post), g(g_ffn_pre), g(g_ffn_post), wa, wb, wu, wd)


def kernel(x_prompt, x_sample, cache_da_k, cache_da_v, cache_nsa_cmp_k, cache_nsa_cmp_v, cache_nsa_sel_k,
           cache_nsa_sel_v, state_nsa_win_k, state_nsa_win_v, page_table, norm_mix_pre, norm_mix_post,
           norm_ffn_pre, norm_ffn_post, w_in, w_out, da_lambda_q1, da_lambda_k1, da_lambda_q2, da_lambda_k2,
           da_subln, cmp_pe_k, cmp_w1_k, cmp_w2_k, cmp_pe_v, cmp_w1_v, cmp_w2_v, w_up, w_down):
    depth = w_in.shape[0]
    n_p, seq = x_prompt.shape[:2]
    nbatch, t = x_sample.shape[:2]
    n_pool = cache_da_k.shape[1]
    past = page_table.shape[1] * PAGE
    assert n_p == 1 and cache_da_k.shape[2] == PAGE
    xp = x_prompt.reshape(seq, D_MODEL)
    xs = x_sample.reshape(nbatch * t, D_MODEL)
    pos_p = jnp.arange(seq)
    pos_s = past + jnp.tile(jnp.arange(t), nbatch)
    w_keep = min(WINDOW, seq)
    p_rows, s_rows = [], []
    for layer in range(depth):
        lam_init = 0.8 - 0.6 * math.exp(-0.3 * layer)
        lams = [a[layer].reshape(1, DA_DIM).astype(F32)
                for a in (da_lambda_q1, da_lambda_k1, da_lambda_q2, da_lambda_k2)]
        subln = da_subln[layer].reshape(1, LANES).astype(F32)
        wp = _prep_w_in(w_in[layer])
        cwk = _prep_cmp(cmp_pe_k[layer], cmp_w1_k[layer], cmp_w2_k[layer])
        cwv = _prep_cmp(cmp_pe_v[layer], cmp_w1_v[layer], cmp_w2_v[layer])
        wa, wb = _prep_w_out(w_out[layer])
        wu, wd = w_up[layer].astype(BF16), w_down[layer].astype(BF16)
        post = (norm_mix_post[layer], norm_ffn_pre[layer], norm_ffn_post[layer], wa, wb, wu, wd)

        p = _project(xp, pos_p, wp, norm_mix_pre[layer], BF16, 512)
        pt = _project_t(xp, pos_p, _prep_w_in_t(w_in[layer]), norm_mix_pre[layer], 512)
        oa_p = _da_prompt(pt, p["ka_b"], lams, subln, lam_init, 512, 1024, 2)
        ck = _compress_prompt(p["kc"], cwk)
        cv = _compress_prompt(p["vc"], cwv)
        ob_p = _nsa_prompt_t(pt, p["ks_b"], p["kw_b"], ck, cv, 512, 512)
        kv_t = lambda a: jnp.transpose(a.reshape(NSA_GROUPS, NSA_DIM, -1), (2, 0, 1))[None]
        p_rows.append((jnp.transpose(pt["kat"].reshape(DA_HEADS, 2, DA_DIM, seq), (3, 0, 1, 2))[None],
                       p["va"].reshape(1, seq, DA_HEADS, 2 * DA_DIM),
                       kv_t(pt["kct"]), kv_t(pt["vct"]), kv_t(pt["kst"]), kv_t(pt["vst"]),
                       kv_t(pt["kwt"][:, seq - w_keep:]), kv_t(pt["vwt"][:, seq - w_keep:])))

        s = _project(xs, pos_s, wp, norm_mix_pre[layer], F32, 512)
        da_k_t = jnp.transpose(cache_da_k[layer], (0, 2, 3, 4, 1)).reshape(n_pool, DA_OUT, PAGE)
        da_v_r = cache_da_v[layer].reshape(n_pool, PAGE * DA_HEADS, 2 * DA_DIM)
        oa_s = _da_sample(s, page_table, da_k_t, da_v_r, lams, subln, lam_init, t)
        pk_t = lambda c: jnp.transpose(c[layer], (0, 2, 3, 1)).reshape(-1, KV_W, c.shape[2])
        cks, cvs = _compress_sample(page_table, pk_t(cache_nsa_cmp_k), pk_t(cache_nsa_cmp_v), cwk, cwv)
        ob_s = _nsa_sample(s, cks, cvs, page_table, pk_t(cache_nsa_sel_k), pk_t(cache_nsa_sel_v),
                           pk_t(state_nsa_win_k), pk_t(state_nsa_win_v), past, t)
        kvs = lambda a: a.reshape(nbatch, t, NSA_GROUPS, NSA_DIM)
        nwk = jnp.concatenate([state_nsa_win_k[layer], kvs(s["kw"])], axis=1)[:, t:]
        nwv = jnp.concatenate([state_nsa_win_v[layer], kvs(s["vw"])], axis=1)[:, t:]
        s_rows.append((s["ka"].reshape(nbatch, t, DA_HEADS, 2, DA_DIM), s["va"].reshape(nbatch, t, DA_HEADS, 2 * DA_DIM),
                       kvs(s["kc"]), kvs(s["vc"]), kvs(s["ks"]), kvs(s["vs"]), nwk, nwv))

        xp = _finish(xp, oa_p, ob_p, *post, 256)
        xs = _finish(xs, oa_s, ob_s, *post, 256)

    p_out = [jnp.stack(list(c), axis=0) for c in zip(*p_rows)]
    s_out = [jnp.stack(list(c), axis=0) for c in zip(*s_rows)]
    return (xp.reshape(1, seq, D_MODEL), xs.reshape(nbatch, t, D_MODEL), *p_out, *s_out)
```
